```python
import math
import jax, jax.numpy as jnp
from jax import lax
import numpy as np

D_MODEL = 1024
BATCH = 16
SEQ = 2048
DEPTH = 4

GRID_W = 64
N_EVEN = (DEPTH + 1) // 2
N_ODD = DEPTH // 2
EPS = 1e-6

DA_HEADS = 4
DA_HEAD_DIM = 64
DA_WIDTH = DA_HEADS * 2 * DA_HEAD_DIM
Q_BLOCK = 128
T5_BUCKETS = 32
T5_MAX_DIST = 128
LRU_WIDTH = D_MODEL // 2
LRU_BLOCKS = 8
LRU_BLOCK = LRU_WIDTH // LRU_BLOCKS
CONV_W = 4
CONV_PAD = (2, 1)
LRU_C = 8.0
EVEN_IN = 4 * DA_WIDTH + 2 * LRU_WIDTH
EVEN_MIX = DA_WIDTH + LRU_WIDTH
NA_HEADS = 16
NA_HEAD_DIM = 64
NA_WIDTH = NA_HEADS * NA_HEAD_DIM
NA_WIN_R = 8
NA_WIN_C = 16
ODD_IN = 4 * NA_WIDTH

kernel_name = "hybrid_diffattn_rglru_natten_encoder"


def rms_norm(x, g):
    xf = x.astype(jnp.float32)
    y = xf * lax.rsqrt(jnp.mean(xf * xf, axis=-1, keepdims=True) + EPS)
    return (y * g.astype(jnp.float32)).astype(x.dtype)


def t5_bucket(rel):
    nb = T5_BUCKETS // 2
    max_exact = nb // 2
    ret = jnp.where(rel > 0, nb, 0)
    n = jnp.abs(rel)
    nf = jnp.maximum(n, 1).astype(jnp.float32)
    large = max_exact + (jnp.log(nf / max_exact) / math.log(T5_MAX_DIST / max_exact)
                         * (nb - max_exact)).astype(jnp.int32)
    large = jnp.minimum(large, nb - 1)
    return ret + jnp.where(n < max_exact, n, large)


def diff_attention(q, k, v, t5_table, lam, lam_init, subln_g):
    B, S = q.shape[0], q.shape[1]
    H, d = DA_HEADS, DA_HEAD_DIM
    scale = d ** -0.5
    n_blk = S // Q_BLOCK
    qh = q.transpose(0, 2, 3, 1, 4)
    kh = k.transpose(0, 2, 3, 1, 4)
    vh = v.transpose(0, 2, 1, 3)
    qb = qh.reshape(B, H, 2, n_blk, Q_BLOCK, d).transpose(3, 0, 1, 2, 4, 5)
    k_pos = jnp.arange(S)

    def block(args):
        q_blk, start = args
        q_pos = start + jnp.arange(Q_BLOCK)
        bias = t5_table[:, t5_bucket(k_pos[None, :] - q_pos[:, None])].astype(jnp.float32)
        logits = jnp.einsum('bhmqd,bhmkd->bhmqk', q_blk, kh,
                            preferred_element_type=jnp.float32) * scale + bias[None, :, None]
        p = jax.nn.softmax(logits, axis=-1)
        attn = p[:, :, 0] - lam * p[:, :, 1]
        return jnp.einsum('bhqk,bhke->bhqe', attn.astype(vh.dtype), vh)

    out = lax.map(block, (qb, jnp.arange(n_blk) * Q_BLOCK))
    out = out.transpose(1, 0, 3, 2, 4).reshape(B, S, H, 2 * d)
    out = rms_norm(out, subln_g) * (1.0 - lam_init)
    return out.reshape(B, S, H * 2 * d)


def bidir_rglru(x, conv_w, conv_b, w_a, b_a, w_x, b_x, lam):
    B, S, W = x.shape
    xc = lax.conv_general_dilated(x, conv_w[:, None, :], window_strides=(1,), padding=[CONV_PAD],
                                  dimension_numbers=('NWC', 'WIO', 'NWC'),
                                  feature_group_count=W) + conv_b
    xb = xc.reshape(B, S, LRU_BLOCKS, LRU_BLOCK)
    za = jnp.einsum('bsnc,rncd->rbsnd', xb, w_a) + b_a.reshape(2, 1, 1, LRU_BLOCKS, LRU_BLOCK)
    zx = jnp.einsum('bsnc,rncd->rbsnd', xb, w_x) + b_x.reshape(2, 1, 1, LRU_BLOCKS, LRU_BLOCK)
    gate_a = jax.nn.sigmoid(za.astype(jnp.float32)).reshape(2, B, S, W)
    gate_x = jax.nn.sigmoid(zx.astype(jnp.float32)).reshape(2, B, S, W)
    log_a = -LRU_C * gate_a * jax.nn.softplus(-lam.astype(jnp.float32))[:, None, None, :]
    a = jnp.exp(log_a)
    mult = jnp.sqrt(-jnp.expm1(2.0 * log_a))
    u = mult * gate_x * xc.astype(jnp.float32)[None]

    def combine(l, r):
        return (l[0] * r[0], r[0] * l[1] + r[1])

    _, h_f = lax.associative_scan(combine, (a[0], u[0]), axis=1)
    _, h_b = lax.associative_scan(combine, (a[1], u[1]), axis=1, reverse=True)
    return (h_f + h_b).astype(x.dtype)


def neighborhood_attention(q, k, v, rpb):
    B, S, H, d = q.shape
    rows = S // GRID_W
    win_r = min(NA_WIN_R, rows)
    scale = d ** -0.5
    qg = q.reshape(B, rows, GRID_W, H, d).transpose(1, 0, 3, 2, 4)
    kg = k.reshape(B, rows, GRID_W, H, d).transpose(0, 3, 1, 2, 4)
    vg = v.reshape(B, rows, GRID_W, H, d).transpose(0, 3, 1, 2, 4)
    cols = jnp.arange(GRID_W)
    col_start = jnp.clip(cols - NA_WIN_C // 2, 0, GRID_W - NA_WIN_C)
    col_mask = (cols[None, :] >= col_start[:, None]) & (cols[None, :] < col_start[:, None] + NA_WIN_C)
    dc_idx = jnp.clip(cols[None, :] - cols[:, None], -(NA_WIN_C - 1), NA_WIN_C - 1) + NA_WIN_C - 1

    def row_block(args):
        q_row, r = args
        rs = jnp.clip(r - NA_WIN_R // 2, 0, rows - win_r)
        k_blk = lax.dynamic_slice_in_dim(kg, rs, win_r, axis=2)
        v_blk = lax.dynamic_slice_in_dim(vg, rs, win_r, axis=2)
        dr_idx = rs + jnp.arange(win_r) - r + NA_WIN_R - 1
        bias = rpb[:, dr_idx[None, :, None], dc_idx[:, None, :]].astype(jnp.float32)
        logits = jnp.einsum('bhqd,bhikd->bhqik', q_row, k_blk,
                            preferred_element_type=jnp.float32) * scale + bias
        logits = jnp.where(col_mask[:, None, :], logits, -jnp.inf)
        p = jax.nn.softmax(logits.reshape(B, H, GRID_W, win_r * GRID_W), axis=-1)
        p = p.reshape(B, H, GRID_W, win_r, GRID_W)
        return jnp.einsum('bhqik,bhikd->bhqd', p.astype(v_blk.dtype), v_blk)

    out = lax.map(row_block, (qg, jnp.arange(rows)))
    return out.transpose(1, 0, 3, 2, 4).reshape(B, S, H * d)


def setup_inputs(seed: int = 0) -> dict:
    key = jax.random.key(seed)
    ks = jax.random.split(key, 24)
    f32 = jnp.float32
    D = D_MODEL
    nrm = lambda k, shape, s: jax.random.normal(k, shape, f32) * s
    a0 = jax.random.uniform(ks[16], (N_EVEN, 2, LRU_WIDTH), f32, 0.9, 0.999)
    p0 = a0 ** (1.0 / LRU_C)
    lru_lambda = jnp.log(p0) - jnp.log1p(-p0)
    return {
        "x": nrm(ks[0], (BATCH, SEQ, D), 1.0),
        "c": nrm(ks[1], (BATCH, D), 1.0),
        "ada_w": nrm(ks[2], (DEPTH, D, 3 * D), 0.5 * D ** -0.5),
        "ada_b": nrm(ks[3], (DEPTH, 3 * D), 0.01),
        "norm_g": 1.0 + nrm(ks[4], (DEPTH, D), 0.02),
        "final_g": 1.0 + nrm(ks[5], (D,), 0.02),
        "t5_table": nrm(ks[6], (DA_HEADS, T5_BUCKETS), 0.5),
        "even_w_in": nrm(ks[7], (N_EVEN, D, EVEN_IN), D ** -0.5),
        "even_w_out": nrm(ks[8], (N_EVEN, EVEN_MIX, D), EVEN_MIX ** -0.5),
        "da_lam": nrm(ks[9], (N_EVEN, 4, DA_HEAD_DIM), 0.1),
        "da_subln_g": 1.0 + nrm(ks[10], (N_EVEN, 2 * DA_HEAD_DIM), 0.02),
        "lru_conv_w": nrm(ks[11], (N_EVEN, CONV_W, LRU_WIDTH), CONV_W ** -0.5),
        "lru_conv_b": nrm(ks[12], (N_EVEN, LRU_WIDTH), 0.01),
        "lru_w_a": nrm(ks[13], (N_EVEN, 2, LRU_BLOCKS, LRU_BLOCK, LRU_BLOCK), LRU_BLOCK ** -0.5),
        "lru_b_a": nrm(ks[14], (N_EVEN, 2, LRU_WIDTH), 0.01),
        "lru_w_x": nrm(ks[15], (N_EVEN, 2, LRU_BLOCKS, LRU_BLOCK, LRU_BLOCK), LRU_BLOCK ** -0.5),
        "lru_b_x": nrm(ks[17], (N_EVEN, 2, LRU_WIDTH), 0.01),
        "lru_lambda": lru_lambda,
        "odd_w_in": nrm(ks[18], (N_ODD, D, ODD_IN), D ** -0.5),
        "odd_w_out": nrm(ks[19], (N_ODD, NA_WIDTH, D), NA_WIDTH ** -0.5),
        "na_rpb": nrm(ks[20], (N_ODD, NA_HEADS, 2 * NA_WIN_R - 1, 2 * NA_WIN_C - 1), 0.3),
    }


def even_mixer(h, w_in, w_out, t5_table, lam_p, lam_init, subln_g,
               conv_w, conv_b, w_a, b_a, w_x, b_x, lru_lambda):
    B, S, _ = h.shape
    proj = h @ w_in
    q, k, v, g_a, x_b, g_b = jnp.split(
        proj, [DA_WIDTH, 2 * DA_WIDTH, 3 * DA_WIDTH, 4 * DA_WIDTH, 4 * DA_WIDTH + LRU_WIDTH], axis=-1)
    q = q.reshape(B, S, DA_HEADS, 2, DA_HEAD_DIM)
    k = k.reshape(B, S, DA_HEADS, 2, DA_HEAD_DIM)
    v = v.reshape(B, S, DA_HEADS, 2 * DA_HEAD_DIM)
    lp = lam_p.astype(jnp.float32)
    lam = jnp.exp(jnp.sum(lp[0] * lp[1])) - jnp.exp(jnp.sum(lp[2] * lp[3])) + lam_init
    a_out = diff_attention(q, k, v, t5_table, lam, lam_init, subln_g)
    b_out = bidir_rglru(x_b, conv_w, conv_b, w_a, b_a, w_x, b_x, lru_lambda)
    mixed = jnp.concatenate([a_out * jax.nn.silu(g_a), b_out * jax.nn.silu(g_b)], axis=-1)
    return mixed @ w_out


def odd_mixer(h, w_in, w_out, rpb):
    B, S, _ = h.shape
    proj = h @ w_in
    q, k, v, g = jnp.split(proj, 4, axis=-1)
    shp = (B, S, NA_HEADS, NA_HEAD_DIM)
    out = neighborhood_attention(q.reshape(shp), k.reshape(shp), v.reshape(shp), rpb)
    return (out * jax.nn.silu(g)) @ w_out


def reference(x, c, ada_w, ada_b, norm_g, final_g, t5_table, even_w_in, even_w_out, da_lam, da_subln_g,
              lru_conv_w, lru_conv_b, lru_w_a, lru_b_a, lru_w_x, lru_b_x, lru_lambda,
              odd_w_in, odd_w_out, na_rpb):
    c_act = jax.nn.silu(c)
    for l in range(DEPTH):
        mod = c_act @ ada_w[l] + ada_b[l]
        shift, scale, gate = jnp.split(mod, 3, axis=-1)
        h = rms_norm(x, norm_g[l]) * (1.0 + scale[:, None, :]) + shift[:, None, :]
        if l % 2 == 0:
            e = l // 2
            lam_init = 0.8 - 0.6 * math.exp(-0.3 * l)
            y = even_mixer(h, even_w_in[e], even_w_out[e], t5_table, da_lam[e], lam_init, da_subln_g[e],
                           lru_conv_w[e], lru_conv_b[e], lru_w_a[e], lru_b_a[e], lru_w_x[e], lru_b_x[e],
                           lru_lambda[e])
        else:
            o = l // 2
            y = odd_mixer(h, odd_w_in[o], odd_w_out[o], na_rpb[o])
        x = x + gate[:, None, :] * y
    return rms_norm(x, final_g)
```

```python
import functools
import math

import numpy as np
import jax
import jax.numpy as jnp
from jax import lax
from jax.experimental import pallas as pl
from jax.experimental.pallas import tpu as pltpu

F32 = jnp.float32
BF16 = jnp.bfloat16

D_MODEL = 1024
BATCH = 16
SEQ = 2048
DEPTH = 4
GRID_W = 64
EPS = 1e-6

DA_HEADS = 4
DA_HEAD_DIM = 64
DA_WIDTH = DA_HEADS * 2 * DA_HEAD_DIM
T5_BUCKETS = 32
T5_MAX_DIST = 128
LRU_WIDTH = D_MODEL // 2
LRU_BLOCKS = 8
LRU_BLOCK = LRU_WIDTH // LRU_BLOCKS
LRU_C = 8.0
NA_HEADS = 16
NA_HEAD_DIM = 64
NA_WIDTH = NA_HEADS * NA_HEAD_DIM
NA_WIN_R = 8
NA_WIN_C = 16
GRID_ROWS = SEQ // GRID_W

LANE = 128
SUBLANE = 8
VMEM_LIMIT = 48 * 1024 * 1024

TM_PROJ = 512
TQ_ATTN = 256
LRU_CH = 256
LRU_TC = 256
N_TILE_DIAG = 2 * (SEQ // LANE) - 1

NT_DIMS = (((1,), (1,)), ((), ()))


def _cparams():
    return pltpu.CompilerParams(vmem_limit_bytes=VMEM_LIMIT)


def _silu(x):
    return x * jax.nn.sigmoid(x)


def _ada_kernel(c_ref, w_ref, b_ref, o_ref):
    ca = _silu(c_ref[...])
    o_ref[0] = jnp.dot(ca, w_ref[0], preferred_element_type=F32) + b_ref[0]


def _ada_mod(c, ada_w, ada_b):
    tn = 1536
    n = 3 * D_MODEL
    return pl.pallas_call(
        _ada_kernel,
        grid=(DEPTH, n // tn),
        in_specs=[
            pl.BlockSpec((BATCH, D_MODEL), lambda l, j: (0, 0)),
            pl.BlockSpec((1, D_MODEL, tn), lambda l, j: (l, 0, j)),
            pl.BlockSpec((1, 1, tn), lambda l, j: (l, 0, j)),
        ],
        out_specs=pl.BlockSpec((1, BATCH, tn), lambda l, j: (l, 0, j)),
        out_shape=jax.ShapeDtypeStruct((DEPTH, BATCH, n), F32),
        compiler_params=_cparams(),
        name="ada_mod",
    )(c, ada_w, ada_b.reshape(DEPTH, 1, n))


def _inproj_kernel(x_ref, mod_ref, g_ref, w_ref, o1_ref, o2_ref, *, n1, n2, chunk):
    x = x_ref[0]
    ms = jnp.mean(x * x, axis=-1, keepdims=True)
    y = x * lax.rsqrt(ms + EPS) * g_ref[...]
    h = (y * (1.0 + mod_ref[0, 1:2, :]) + mod_ref[0, 0:1, :]).astype(BF16)
    for j in range(0, n1, chunk):
        o1_ref[0, :, j:j + chunk] = jnp.dot(
            h, w_ref[:, j:j + chunk], preferred_element_type=F32).astype(o1_ref.dtype)
    for j in range(0, n2, chunk):
        o2_ref[0, :, j:j + chunk] = jnp.dot(
            h, w_ref[:, n1 + j:n1 + j + chunk], preferred_element_type=F32).astype(o2_ref.dtype)


def _inproj(x, mod3, g, w_bf16, n1, n2):
    kern = functools.partial(_inproj_kernel, n1=n1, n2=n2, chunk=512)
    return pl.pallas_call(
        kern,
        grid=(BATCH, SEQ // TM_PROJ),
        in_specs=[
            pl.BlockSpec((1, TM_PROJ, D_MODEL), lambda b, i: (b, i, 0)),
            pl.BlockSpec((1, 3, D_MODEL), lambda b, i: (b, 0, 0)),
            pl.BlockSpec((1, D_MODEL), lambda b, i: (0, 0)),
            pl.BlockSpec((D_MODEL, n1 + n2), lambda b, i: (0, 0)),
        ],
        out_specs=[
            pl.BlockSpec((1, TM_PROJ, n1), lambda b, i: (b, i, 0)),
            pl.BlockSpec((1, TM_PROJ, n2), lambda b, i: (b, i, 0)),
        ],
        out_shape=[
            jax.ShapeDtypeStruct((BATCH, SEQ, n1), BF16),
            jax.ShapeDtypeStruct((BATCH, SEQ, n2), F32),
        ],
        compiler_params=_cparams(),
        name="inproj",
    )(x, mod3, g.reshape(1, D_MODEL), w_bf16)


def _outproj_kernel(*refs, widths, final):
    n_in = len(widths)
    m_refs = refs[:n_in]
    w_ref, x_ref, mod_ref = refs[n_in:n_in + 3]
    rest = refs[n_in + 3:]
    if final:
        fg_ref, o_ref = rest
    else:
        (o_ref,) = rest
    y = None
    off = 0
    for m_ref, wd in zip(m_refs, widths):
        part = jnp.dot(m_ref[0], w_ref[off:off + wd, :], preferred_element_type=F32)
        y = part if y is None else y + part
        off += wd
    xn = x_ref[0] + mod_ref[0, 2:3, :] * y
    if final:
        ms = jnp.mean(xn * xn, axis=-1, keepdims=True)
        xn = xn * lax.rsqrt(ms + EPS) * fg_ref[...]
    o_ref[0] = xn


def _outproj(mixed, w_bf16, x, mod3, final_g=None):
    widths = tuple(m.shape[-1] for m in mixed)
    final = final_g is not None
    kern = functools.partial(_outproj_kernel, widths=widths, final=final)
    in_specs = [pl.BlockSpec((1, TM_PROJ, wd), lambda b, i: (b, i, 0)) for wd in widths]
    in_specs += [
        pl.BlockSpec((sum(widths), D_MODEL), lambda b, i: (0, 0)),
        pl.BlockSpec((1, TM_PROJ, D_MODEL), lambda b, i: (b, i, 0)),
        pl.BlockSpec((1, 3, D_MODEL), lambda b, i: (b, 0, 0)),
    ]
    args = list(mixed) + [w_bf16, x, mod3]
    if final:
        in_specs.append(pl.BlockSpec((1, D_MODEL), lambda b, i: (0, 0)))
        args.append(final_g.reshape(1, D_MODEL))
    return pl.pallas_call(
        kern,
        grid=(BATCH, SEQ // TM_PROJ),
        in_specs=in_specs,
        out_specs=pl.BlockSpec((1, TM_PROJ, D_MODEL), lambda b, i: (b, i, 0)),
        out_shape=jax.ShapeDtypeStruct((BATCH, SEQ, D_MODEL), F32),
        compiler_params=_cparams(),
        name="outproj_final" if final else "outproj",
    )(*args)


def _t5_bucket_np(rel):
    nb = T5_BUCKETS // 2
    max_exact = nb // 2
    ret = np.where(rel > 0, nb, 0)
    n = np.abs(rel)
    nf = np.maximum(n, 1).astype(np.float32)
    large = max_exact + (np.log(nf / np.float32(max_exact)) / np.float32(math.log(T5_MAX_DIST / max_exact))
                         * np.float32(nb - max_exact)).astype(np.int32)
    large = np.minimum(large, nb - 1)
    return (ret + np.where(n < max_exact, n, large)).astype(np.int32)


def _t5_tile_buckets():
    d = np.arange(N_TILE_DIAG)[:, None, None] - (SEQ // LANE - 1)
    i = np.arange(LANE)[None, :, None]
    j = np.arange(LANE)[None, None, :]
    return _t5_bucket_np(d * LANE + j - i)


def _t5_tiles_kernel(tab_ref, idx_ref, o_ref):
    h = pl.program_id(0)

    def body(d, carry):
        idx = idx_ref[d]
        val = jnp.full(idx.shape, tab_ref[h, 0], F32)
        for j in range(1, T5_BUCKETS):
            val = jnp.where(idx == j, tab_ref[h, j], val)
        o_ref[0, d] = val
        return carry

    lax.fori_loop(0, N_TILE_DIAG, body, 0)


def _t5_tiles(t5_table):
    idx = jnp.asarray(_t5_tile_buckets())
    return pl.pallas_call(
        _t5_tiles_kernel,
        grid=(DA_HEADS,),
        in_specs=[
            pl.BlockSpec(memory_space=pltpu.SMEM),
            pl.BlockSpec((N_TILE_DIAG, LANE, LANE), lambda h: (0, 0, 0)),
        ],
        out_specs=pl.BlockSpec((1, N_TILE_DIAG, LANE, LANE), lambda h: (h, 0, 0, 0)),
        out_shape=jax.ShapeDtypeStruct((DA_HEADS, N_TILE_DIAG, LANE, LANE), F32),
        compiler_params=_cparams(),
        name="t5_tiles",
    )(t5_table, idx)


def _dattn_kernel(lam_ref, q_ref, k_ref, v_ref, t_ref, ga_ref, sg_ref, o_ref, *, lam_init):
    qi = pl.program_id(1)
    lp = lam_ref[...]
    l1 = jnp.sum(lp[0:1] * lp[1:2], axis=-1, keepdims=True)
    l2 = jnp.sum(lp[2:3] * lp[3:4], axis=-1, keepdims=True)
    lam = jnp.exp(l1) - jnp.exp(l2) + lam_init

    q = q_ref[0] * jnp.asarray(DA_HEAD_DIM ** -0.5, BF16)
    lane = lax.broadcasted_iota(jnp.int32, q.shape, 1)
    zero = jnp.zeros_like(q)
    q1 = jnp.where(lane < DA_HEAD_DIM, q, zero)
    q2 = jnp.where(lane >= DA_HEAD_DIM, q, zero)
    k = k_ref[0]
    v = v_ref[0]

    n_kb = SEQ // LANE
    rows = []
    for rb in range(TQ_ATTN // LANE):
        qb = qi * (TQ_ATTN // LANE) + rb
        rows.append(jnp.concatenate(
            [t_ref[0, kb - qb + (n_kb - 1)] for kb in range(n_kb)], axis=1))
    bias = jnp.concatenate(rows, axis=0)

    def softmax_parts(qm):
        s = lax.dot_general(qm, k, NT_DIMS, preferred_element_type=F32) + bias
        m = jnp.max(s, axis=-1, keepdims=True)
        e = jnp.exp(s - m)
        return e, 1.0 / jnp.sum(e, axis=-1, keepdims=True)

    e1, r1 = softmax_parts(q1)
    e2, r2 = softmax_parts(q2)
    a = e1 * r1 - e2 * (lam * r2)
    o = jnp.dot(a.astype(BF16), v, preferred_element_type=F32)
    ms = jnp.mean(o * o, axis=-1, keepdims=True)
    o = o * lax.rsqrt(ms + EPS) * sg_ref[...] * (1.0 - lam_init)
    o_ref[0] = (o * _silu(ga_ref[0])).astype(o_ref.dtype)


def _dattn(qkv, rest, tiles, lam_p, subln_g, lam_init):
    kern = functools.partial(_dattn_kernel, lam_init=lam_init)
    nh = DA_HEADS
    return pl.pallas_call(
        kern,
        grid=(nh, SEQ // TQ_ATTN, BATCH),
        in_specs=[
            pl.BlockSpec((4, DA_HEAD_DIM), lambda h, i, b: (0, 0)),
            pl.BlockSpec((1, TQ_ATTN, LANE), lambda h, i, b: (b, i, h)),
            pl.BlockSpec((1, SEQ, LANE), lambda h, i, b: (b, 0, nh + h)),
            pl.BlockSpec((1, SEQ, LANE), lambda h, i, b: (b, 0, 2 * nh + h)),
            pl.BlockSpec((1, N_TILE_DIAG, LANE, LANE), lambda h, i, b: (h, 0, 0, 0)),
            pl.BlockSpec((1, TQ_ATTN, LANE), lambda h, i, b: (b, i, h)),
            pl.BlockSpec((1, LANE), lambda h, i, b: (0, 0)),
        ],
        out_specs=pl.BlockSpec((1, TQ_ATTN, LANE), lambda h, i, b: (b, i, h)),
        out_shape=jax.ShapeDtypeStruct((BATCH, SEQ, DA_WIDTH), BF16),
        compiler_params=_cparams(),
        name="diff_attn",
    )(lam_p, qkv, qkv, qkv, tiles, rest, subln_g.reshape(1, LANE))


def _lru_kernel(x_ref, g_ref, cw_ref, cb_ref, w_ref, b_ref, lam_ref, o_ref,
                xc_ref, a3, u3, car_ref, h_ref):
    S, C = SEQ, LRU_CH
    n_tiles = S // SUBLANE
    tiles_per_chunk = LRU_TC // SUBLANE

    x = x_ref[0]
    row = lax.broadcasted_iota(jnp.int32, (S, C), 0)
    xm2 = jnp.where(row >= 2, pltpu.roll(x, 2, 0), 0.0)
    xm1 = jnp.where(row >= 1, pltpu.roll(x, 1, 0), 0.0)
    xp1 = jnp.where(row < S - 1, pltpu.roll(x, S - 1, 0), 0.0)
    xc_ref[...] = (cw_ref[0:1, :] * xm2 + cw_ref[1:2, :] * xm1 + cw_ref[2:3, :] * x
                   + cw_ref[3:4, :] * xp1 + cb_ref[...])

    row8 = lax.broadcasted_iota(jnp.int32, (LRU_TC, C), 0) % SUBLANE

    for direction in range(2):
        reverse = direction == 1
        lam = lam_ref[direction:direction + 1, :]
        neg_c_sp = -LRU_C * (jnp.maximum(-lam, 0.0) + jnp.log1p(jnp.exp(-jnp.abs(lam))))

        def chunk_body(ci, carry):
            t0 = pl.multiple_of(ci * LRU_TC, LRU_TC)
            xc = xc_ref[pl.ds(t0, LRU_TC), :]
            z = jnp.dot(xc.astype(BF16), w_ref[0, :, direction * 2 * C:(direction + 1) * 2 * C],
                        preferred_element_type=F32) + b_ref[0, :, direction * 2 * C:(direction + 1) * 2 * C]
            log_a = neg_c_sp * jax.nn.sigmoid(z[:, :C])
            a = jnp.exp(log_a)
            th = jnp.tanh(log_a)
            mult = jnp.sqrt(-2.0 * th / (1.0 - th))
            u = mult * jax.nn.sigmoid(z[:, C:]) * xc
            for d in (1, 2, 4):
                if reverse:
                    ar = pltpu.roll(a, LRU_TC - d, 0)
                    ur = pltpu.roll(u, LRU_TC - d, 0)
                    m = row8 < SUBLANE - d
                else:
                    ar = pltpu.roll(a, d, 0)
                    ur = pltpu.roll(u, d, 0)
                    m = row8 >= d
                u = jnp.where(m, a * ur + u, u)
                a = jnp.where(m, a * ar, a)
            j0 = pl.multiple_of(ci * tiles_per_chunk, tiles_per_chunk)
            a3[pl.ds(j0, tiles_per_chunk)] = a.reshape(tiles_per_chunk, SUBLANE, C)
            u3[pl.ds(j0, tiles_per_chunk)] = u.reshape(tiles_per_chunk, SUBLANE, C)
            return carry

        lax.fori_loop(0, S // LRU_TC, chunk_body, 0)

        edge = 0 if reverse else SUBLANE - 1

        def carry_body(jj, c):
            j = (n_tiles - 1 - jj) if reverse else jj
            car_ref[j] = c
            return u3[j, edge:edge + 1, :] + a3[j, edge:edge + 1, :] * c

        lax.fori_loop(0, n_tiles, carry_body, jnp.zeros((1, C), F32))

        hdir = u3[...] + a3[...] * car_ref[...]
        if reverse:
            h_ref[...] = h_ref[...] + hdir
        else:
            h_ref[...] = hdir

    hsum = h_ref[...].reshape(S, C)
    o_ref[0] = (hsum * _silu(g_ref[0])).astype(o_ref.dtype)


def _lru_weights(w_a, b_a, w_x, b_x):
    n_grp = LRU_WIDTH // LRU_CH
    bpg = LRU_CH // LRU_BLOCK

    def dense(w):
        w = w.reshape(n_grp, bpg, LRU_BLOCK, LRU_BLOCK)
        eye = jnp.eye(bpg, dtype=w.dtype)
        return jnp.einsum('gncd,nm->gncmd', w, eye).reshape(n_grp, LRU_CH, LRU_CH)

    w = jnp.concatenate([dense(w_a[0]), dense(w_x[0]), dense(w_a[1]), dense(w_x[1])], axis=-1)
    grp = lambda b: b.reshape(n_grp, 1, LRU_CH)
    b = jnp.concatenate([grp(b_a[0]), grp(b_x[0]), grp(b_a[1]), grp(b_x[1])], axis=-1)
    return w.astype(BF16), b.astype(F32)


def _lru(rest, conv_w, conv_b, w_a, b_a, w_x, b_x, lru_lambda):
    w, b = _lru_weights(w_a, b_a, w_x, b_x)
    n_grp = LRU_WIDTH // LRU_CH
    xb0 = LRU_WIDTH // LRU_CH
    gb0 = 2 * LRU_WIDTH // LRU_CH
    return pl.pallas_call(
        _lru_kernel,
        grid=(BATCH, n_grp),
        in_specs=[
            pl.BlockSpec((1, SEQ, LRU_CH), lambda bb, c: (bb, 0, xb0 + c)),
            pl.BlockSpec((1, SEQ, LRU_CH), lambda bb, c: (bb, 0, gb0 + c)),
            pl.BlockSpec((4, LRU_CH), lambda bb, c: (0, c)),
            pl.BlockSpec((1, LRU_CH), lambda bb, c: (0, c)),
            pl.BlockSpec((1, LRU_CH, 4 * LRU_CH), lambda bb, c: (c, 0, 0)),
            pl.BlockSpec((1, 1, 4 * LRU_CH), lambda bb, c: (c, 0, 0)),
            pl.BlockSpec((2, LRU_CH), lambda bb, c: (0, c)),
        ],
        out_specs=pl.BlockSpec((1, SEQ, LRU_CH), lambda bb, c: (bb, 0, c)),
        out_shape=jax.ShapeDtypeStruct((BATCH, SEQ, LRU_WIDTH), BF16),
        scratch_shapes=[
            pltpu.VMEM((SEQ, LRU_CH), F32),
            pltpu.VMEM((SEQ // SUBLANE, SUBLANE, LRU_CH), F32),
            pltpu.VMEM((SEQ // SUBLANE, SUBLANE, LRU_CH), F32),
            pltpu.VMEM((SEQ // SUBLANE, 1, LRU_CH), F32),
            pltpu.VMEM((SEQ // SUBLANE, SUBLANE, LRU_CH), F32),
        ],
        compiler_params=_cparams(),
        name="rglru",
    )(rest, rest, conv_w, conv_b.reshape(1, LRU_WIDTH), w, b, lru_lambda)


def _na_rpb_kernel(rpb_ref, dc_ref, o_ref):
    dc = dc_ref[...]
    n_dc = 2 * NA_WIN_C - 1
    val = jnp.broadcast_to(rpb_ref[:, 0:1], o_ref.shape)
    for j in range(1, n_dc):
        val = jnp.where(dc == j, rpb_ref[:, j:j + 1], val)
    o_ref[...] = val


def _na_bias(rpb):
    n_dr = 2 * NA_WIN_R - 1
    n_dc = 2 * NA_WIN_C - 1
    cols = np.arange(GRID_W)
    dc_idx = (np.clip(cols[None, :] - cols[:, None], -(NA_WIN_C - 1), NA_WIN_C - 1)
              + NA_WIN_C - 1).astype(np.int32)
    toe = pl.pallas_call(
        _na_rpb_kernel,
        out_shape=jax.ShapeDtypeStruct((NA_HEADS * n_dr, GRID_W * GRID_W), F32),
        compiler_params=_cparams(),
        name="na_rpb",
    )(rpb.reshape(NA_HEADS * n_dr, n_dc), jnp.asarray(dc_idx.reshape(1, -1)))
    toe = toe.reshape(NA_HEADS, n_dr, GRID_W, GRID_W)
    col_start = np.clip(cols - NA_WIN_C // 2, 0, GRID_W - NA_WIN_C)
    col_mask = (cols[None, :] >= col_start[:, None]) & (cols[None, :] < col_start[:, None] + NA_WIN_C)
    neg = jnp.asarray(np.where(col_mask, 0.0, -np.inf).astype(np.float32))
    per_cls = []
    for cls in range(NA_WIN_R):
        blk = toe[:, NA_WIN_R - 1 - cls:2 * NA_WIN_R - 1 - cls] + neg[None, None]
        per_cls.append(blk.transpose(0, 2, 1, 3).reshape(NA_HEADS, GRID_W, NA_WIN_R * GRID_W))
    bias = jnp.stack(per_cls, axis=1)
    bias = bias.reshape(NA_HEADS // 2, 2, NA_WIN_R, GRID_W, NA_WIN_R * GRID_W)
    return bias.transpose(0, 2, 1, 3, 4).reshape(NA_HEADS // 2, 2 * NA_WIN_R, GRID_W, NA_WIN_R * GRID_W)


def _na_kernel(q_ref, k_ref, v_ref, g_ref, bias_ref, o_ref):
    win = NA_WIN_R * GRID_W
    scale = jnp.asarray(NA_HEAD_DIM ** -0.5, BF16)
    lane = lax.broadcasted_iota(jnp.int32, (GRID_W, LANE), 1)
    lo = lane < NA_HEAD_DIM

    def row_body(r, carry):
        rs = jnp.clip(r - NA_WIN_R // 2, 0, GRID_ROWS - NA_WIN_R)
        cls = r - rs
        k0 = pl.multiple_of(rs * GRID_W, GRID_W)
        q0 = pl.multiple_of(r * GRID_W, GRID_W)
        kw = k_ref[0, pl.ds(k0, win), :]
        vw = v_ref[0, pl.ds(k0, win), :]
        q = q_ref[0, pl.ds(q0, GRID_W), :] * scale
        zero = jnp.zeros_like(q)
        outs = []
        for par in range(2):
            qm = jnp.where(lo, q, zero) if par == 0 else jnp.where(lo, zero, q)
            s = lax.dot_general(qm, kw, NT_DIMS, preferred_element_type=F32) + bias_ref[0, 2 * cls + par]
            m = jnp.max(s, axis=-1, keepdims=True)
            e = jnp.exp(s - m)
            p = e * (1.0 / jnp.sum(e, axis=-1, keepdims=True))
            outs.append(jnp.dot(p.astype(BF16), vw, preferred_element_type=F32))
        o = jnp.where(lo, outs[0], outs[1])
        o_ref[0, pl.ds(q0, GRID_W), :] = (o * _silu(g_ref[0, pl.ds(q0, GRID_W), :])).astype(o_ref.dtype)
        return carry

    lax.fori_loop(0, GRID_ROWS, row_body, 0)


def _na(qkv, gate, bias):
    n_pair = NA_HEADS // 2
    return pl.pallas_call(
        _na_kernel,
        grid=(BATCH, n_pair),
        in_specs=[
            pl.BlockSpec((1, SEQ, LANE), lambda b, p: (b, 0, p)),
            pl.BlockSpec((1, SEQ, LANE), lambda b, p: (b, 0, n_pair + p)),
            pl.BlockSpec((1, SEQ, LANE), lambda b, p: (b, 0, 2 * n_pair + p)),
            pl.BlockSpec((1, SEQ, LANE), lambda b, p: (b, 0, p)),
            pl.BlockSpec((1, 2 * NA_WIN_R, GRID_W, NA_WIN_R * GRID_W), lambda b, p: (p, 0, 0, 0)),
        ],
        out_specs=pl.BlockSpec((1, SEQ, LANE), lambda b, p: (b, 0, p)),
        out_shape=jax.ShapeDtypeStruct((BATCH, SEQ, NA_WIDTH), BF16),
        compiler_params=_cparams(),
        name="nbr_attn",
    )(qkv, qkv, qkv, gate, bias)


def kernel(x, c, ada_w, ada_b, norm_g, final_g, t5_table, even_w_in, even_w_out, da_lam, da_subln_g,
           lru_conv_w, lru_conv_b, lru_w_a, lru_b_a, lru_w_x, lru_b_x, lru_lambda,
           odd_w_in, odd_w_out, na_rpb):
    mod = _ada_mod(c, ada_w, ada_b)
    tiles = _t5_tiles(t5_table)
    for l in range(DEPTH):
        mod3 = mod[l].reshape(BATCH, 3, D_MODEL)
        last = final_g if l == DEPTH - 1 else None
        if l % 2 == 0:
            e = l // 2
            lam_init = 0.8 - 0.6 * math.exp(-0.3 * l)
            qkv, rest = _inproj(x, mod3, norm_g[l], even_w_in[e].astype(BF16), 3 * DA_WIDTH,
                                DA_WIDTH + 2 * LRU_WIDTH)
            mix_a = _dattn(qkv, rest, tiles, da_lam[e], da_subln_g[e], lam_init)
            mix_b = _lru(rest, lru_conv_w[e], lru_conv_b[e], lru_w_a[e], lru_b_a[e], lru_w_x[e],
                         lru_b_x[e], lru_lambda[e])
            x = _outproj([mix_a, mix_b], even_w_out[e].astype(BF16), x, mod3, last)
        else:
            o = l // 2
            qkv, gate = _inproj(x, mod3, norm_g[l], odd_w_in[o].astype(BF16), 3 * NA_WIDTH, NA_WIDTH)
            mix = _na(qkv, gate, _na_bias(na_rpb[o]))
            x = _outproj([mix], odd_w_out[o].astype(BF16), x, mod3, last)
    return x
```

```python
import functools
import math

import numpy as np
import jax
import jax.numpy as jnp
from jax import lax
from jax.experimental import pallas as pl
from jax.experimental.pallas import tpu as pltpu

F32 = jnp.float32
BF16 = jnp.bfloat16

D_MODEL = 1024
BATCH = 16
SEQ = 2048
DEPTH = 4
GRID_W = 64
EPS = 1e-6

DA_HEADS = 4
DA_HEAD_DIM = 64
DA_WIDTH = DA_HEADS * 2 * DA_HEAD_DIM
T5_BUCKETS = 32
T5_MAX_DIST = 128
LRU_WIDTH = D_MODEL // 2
LRU_BLOCKS = 8
LRU_BLOCK = LRU_WIDTH // LRU_BLOCKS
LRU_C = 8.0
NA_HEADS = 16
NA_HEAD_DIM = 64
NA_WIDTH = NA_HEADS * NA_HEAD_DIM
NA_WIN_R = 8
NA_WIN_C = 16
GRID_ROWS = SEQ // GRID_W

LANE = 128
SUBLANE = 8
VMEM_LIMIT = 48 * 1024 * 1024

TM_PROJ = 512
TQ_ATTN = 256
LRU_CH = 256
LRU_TC = 256
N_TILE_DIAG = 2 * (SEQ // LANE) - 1
NA_ROWS_PER_STEP = 4

NT_DIMS = (((1,), (1,)), ((), ()))


def _cparams():
    return pltpu.CompilerParams(vmem_limit_bytes=VMEM_LIMIT)


def _silu(x):
    return x * jax.nn.sigmoid(x)


def _ada_kernel(c_ref, w_ref, b_ref, o_ref):
    ca = _silu(c_ref[...])
    o_ref[0] = jnp.dot(ca, w_ref[0], preferred_element_type=F32) + b_ref[0]


def _ada_mod(c, ada_w, ada_b):
    tn = 1536
    n = 3 * D_MODEL
    return pl.pallas_call(
        _ada_kernel,
        grid=(DEPTH, n // tn),
        in_specs=[
            pl.BlockSpec((BATCH, D_MODEL), lambda l, j: (0, 0)),
            pl.BlockSpec((1, D_MODEL, tn), lambda l, j: (l, 0, j)),
            pl.BlockSpec((1, 1, tn), lambda l, j: (l, 0, j)),
        ],
        out_specs=pl.BlockSpec((1, BATCH, tn), lambda l, j: (l, 0, j)),
        out_shape=jax.ShapeDtypeStruct((DEPTH, BATCH, n), F32),
        compiler_params=_cparams(),
        name="ada_mod",
    )(c, ada_w, ada_b.reshape(DEPTH, 1, n))


def _inproj_kernel(x_ref, mod_ref, g_ref, w_ref, o1_ref, o2_ref, *, n1, n2, chunk):
    x = x_ref[0]
    ms = jnp.mean(x * x, axis=-1, keepdims=True)
    y = x * lax.rsqrt(ms + EPS) * g_ref[...]
    h = (y * (1.0 + mod_ref[0, 1:2, :]) + mod_ref[0, 0:1, :]).astype(BF16)
    for j in range(0, n1, chunk):
        o1_ref[0, :, j:j + chunk] = jnp.dot(
            h, w_ref[:, j:j + chunk], preferred_element_type=F32).astype(o1_ref.dtype)
    for j in range(0, n2, chunk):
        o2_ref[0, :, j:j + chunk] = jnp.dot(
            h, w_ref[:, n1 + j:n1 + j + chunk], preferred_element_type=F32).astype(o2_ref.dtype)


def _inproj(x, mod3, g, w_bf16, n1, n2):
    kern = functools.partial(_inproj_kernel, n1=n1, n2=n2, chunk=512)
    return pl.pallas_call(
        kern,
        grid=(BATCH, SEQ // TM_PROJ),
        in_specs=[
            pl.BlockSpec((1, TM_PROJ, D_MODEL), lambda b, i: (b, i, 0)),
            pl.BlockSpec((1, 3, D_MODEL), lambda b, i: (b, 0, 0)),
            pl.BlockSpec((1, D_MODEL), lambda b, i: (0, 0)),
            pl.BlockSpec((D_MODEL, n1 + n2), lambda b, i: (0, 0)),
        ],
        out_specs=[
            pl.BlockSpec((1, TM_PROJ, n1), lambda b, i: (b, i, 0)),
            pl.BlockSpec((1, TM_PROJ, n2), lambda b, i: (b, i, 0)),
        ],
        out_shape=[
            jax.ShapeDtypeStruct((BATCH, SEQ, n1), BF16),
            jax.ShapeDtypeStruct((BATCH, SEQ, n2), F32),
        ],
        compiler_params=_cparams(),
        name="inproj",
    )(x, mod3, g.reshape(1, D_MODEL), w_bf16)


def _outproj_kernel(*refs, widths, final):
    n_in = len(widths)
    m_refs = refs[:n_in]
    w_ref, x_ref, mod_ref = refs[n_in:n_in + 3]
    rest = refs[n_in + 3:]
    if final:
        fg_ref, o_ref = rest
    else:
        (o_ref,) = rest
    y = None
    off = 0
    for m_ref, wd in zip(m_refs, widths):
        part = jnp.dot(m_ref[0], w_ref[off:off + wd, :], preferred_element_type=F32)
        y = part if y is None else y + part
        off += wd
    xn = x_ref[0] + mod_ref[0, 2:3, :] * y
    if final:
        ms = jnp.mean(xn * xn, axis=-1, keepdims=True)
        xn = xn * lax.rsqrt(ms + EPS) * fg_ref[...]
    o_ref[0] = xn


def _outproj(mixed, w_bf16, x, mod3, final_g=None):
    widths = tuple(m.shape[-1] for m in mixed)
    final = final_g is not None
    kern = functools.partial(_outproj_kernel, widths=widths, final=final)
    in_specs = [pl.BlockSpec((1, TM_PROJ, wd), lambda b, i: (b, i, 0)) for wd in widths]
    in_specs += [
        pl.BlockSpec((sum(widths), D_MODEL), lambda b, i: (0, 0)),
        pl.BlockSpec((1, TM_PROJ, D_MODEL), lambda b, i: (b, i, 0)),
        pl.BlockSpec((1, 3, D_MODEL), lambda b, i: (b, 0, 0)),
    ]
    args = list(mixed) + [w_bf16, x, mod3]
    if final:
        in_specs.append(pl.BlockSpec((1, D_MODEL), lambda b, i: (0, 0)))
        args.append(final_g.reshape(1, D_MODEL))
    return pl.pallas_call(
        kern,
        grid=(BATCH, SEQ // TM_PROJ),
        in_specs=in_specs,
        out_specs=pl.BlockSpec((1, TM_PROJ, D_MODEL), lambda b, i: (b, i, 0)),
        out_shape=jax.ShapeDtypeStruct((BATCH, SEQ, D_MODEL), F32),
        compiler_params=_cparams(),
        name="outproj_final" if final else "outproj",
    )(*args)


def _t5_bucket_np(rel):
    nb = T5_BUCKETS // 2
    max_exact = nb // 2
    ret = np.where(rel > 0, nb, 0)
    n = np.abs(rel)
    nf = np.maximum(n, 1).astype(np.float32)
    large = max_exact + (np.log(nf / np.float32(max_exact)) / np.float32(math.log(T5_MAX_DIST / max_exact))
                         * np.float32(nb - max_exact)).astype(np.int32)
    large = np.minimum(large, nb - 1)
    return (ret + np.where(n < max_exact, n, large)).astype(np.int32)


def _t5_tile_buckets():
    d = np.arange(N_TILE_DIAG)[:, None, None] - (SEQ // LANE - 1)
    i = np.arange(LANE)[None, :, None]
    j = np.arange(LANE)[None, None, :]
    return _t5_bucket_np(d * LANE + j - i)


def _t5_tiles_kernel(tab_ref, idx_ref, o_ref):
    h = pl.program_id(0)

    def body(d, carry):
        idx = idx_ref[d]
        val = jnp.full(idx.shape, tab_ref[h, 0], F32)
        for j in range(1, T5_BUCKETS):
            val = jnp.where(idx == j, tab_ref[h, j], val)
        o_ref[0, d] = val
        return carry

    lax.fori_loop(0, N_TILE_DIAG, body, 0)


def _t5_tiles(t5_table):
    idx = jnp.asarray(_t5_tile_buckets())
    return pl.pallas_call(
        _t5_tiles_kernel,
        grid=(DA_HEADS,),
        in_specs=[
            pl.BlockSpec(memory_space=pltpu.SMEM),
            pl.BlockSpec((N_TILE_DIAG, LANE, LANE), lambda h: (0, 0, 0)),
        ],
        out_specs=pl.BlockSpec((1, N_TILE_DIAG, LANE, LANE), lambda h: (h, 0, 0, 0)),
        out_shape=jax.ShapeDtypeStruct((DA_HEADS, N_TILE_DIAG, LANE, LANE), F32),
        compiler_params=_cparams(),
        name="t5_tiles",
    )(t5_table, idx)


def _dattn_kernel(lam_ref, q_ref, k_ref, v_ref, t_ref, ga_ref, sg_ref, o_ref, *, lam_init):
    qi = pl.program_id(1)
    lp = lam_ref[...]
    l1 = jnp.sum(lp[0:1] * lp[1:2], axis=-1, keepdims=True)
    l2 = jnp.sum(lp[2:3] * lp[3:4], axis=-1, keepdims=True)
    lam = jnp.exp(l1) - jnp.exp(l2) + lam_init

    q = q_ref[0] * jnp.asarray(DA_HEAD_DIM ** -0.5, BF16)
    lane = lax.broadcasted_iota(jnp.int32, q.shape, 1)
    zero = jnp.zeros_like(q)
    q1 = jnp.where(lane < DA_HEAD_DIM, q, zero)
    q2 = jnp.where(lane >= DA_HEAD_DIM, q, zero)
    k = k_ref[0]
    v = v_ref[0]

    n_kb = SEQ // LANE
    rows = []
    for rb in range(TQ_ATTN // LANE):
        qb = qi * (TQ_ATTN // LANE) + rb
        rows.append(jnp.concatenate(
            [t_ref[0, kb - qb + (n_kb - 1)] for kb in range(n_kb)], axis=1))
    bias = jnp.concatenate(rows, axis=0)

    def softmax_parts(qm):
        s = lax.dot_general(qm, k, NT_DIMS, preferred_element_type=F32) + bias
        m = jnp.max(s, axis=-1, keepdims=True)
        e = jnp.exp(s - m)
        return e, 1.0 / jnp.sum(e, axis=-1, keepdims=True)

    e1, r1 = softmax_parts(q1)
    e2, r2 = softmax_parts(q2)
    a = e1 * r1 - e2 * (lam * r2)
    o = jnp.dot(a.astype(BF16), v, preferred_element_type=F32)
    ms = jnp.mean(o * o, axis=-1, keepdims=True)
    o = o * lax.rsqrt(ms + EPS) * sg_ref[...] * (1.0 - lam_init)
    o_ref[0] = (o * _silu(ga_ref[0])).astype(o_ref.dtype)


def _dattn(qkv, rest, tiles, lam_p, subln_g, lam_init):
    kern = functools.partial(_dattn_kernel, lam_init=lam_init)
    nh = DA_HEADS
    return pl.pallas_call(
        kern,
        grid=(nh, SEQ // TQ_ATTN, BATCH),
        in_specs=[
            pl.BlockSpec((4, DA_HEAD_DIM), lambda h, i, b: (0, 0)),
            pl.BlockSpec((1, TQ_ATTN, LANE), lambda h, i, b: (b, i, h)),
            pl.BlockSpec((1, SEQ, LANE), lambda h, i, b: (b, 0, nh + h)),
            pl.BlockSpec((1, SEQ, LANE), lambda h, i, b: (b, 0, 2 * nh + h)),
            pl.BlockSpec((1, N_TILE_DIAG, LANE, LANE), lambda h, i, b: (h, 0, 0, 0)),
            pl.BlockSpec((1, TQ_ATTN, LANE), lambda h, i, b: (b, i, h)),
            pl.BlockSpec((1, LANE), lambda h, i, b: (0, 0)),
        ],
        out_specs=pl.BlockSpec((1, TQ_ATTN, LANE), lambda h, i, b: (b, i, h)),
        out_shape=jax.ShapeDtypeStruct((BATCH, SEQ, DA_WIDTH), BF16),
        compiler_params=_cparams(),
        name="diff_attn",
    )(lam_p, qkv, qkv, qkv, tiles, rest, subln_g.reshape(1, LANE))


def _lru_kernel(x_ref, g_ref, cw_ref, cb_ref, w_ref, b_ref, lam_ref, o_ref,
                xc_ref, a3, u3, car_ref, h_ref):
    S, C = SEQ, LRU_CH
    n_tiles = S // SUBLANE
    tiles_per_chunk = LRU_TC // SUBLANE

    x = x_ref[0]
    row = lax.broadcasted_iota(jnp.int32, (S, C), 0)
    xm2 = jnp.where(row >= 2, pltpu.roll(x, 2, 0), 0.0)
    xm1 = jnp.where(row >= 1, pltpu.roll(x, 1, 0), 0.0)
    xp1 = jnp.where(row < S - 1, pltpu.roll(x, S - 1, 0), 0.0)
    xc_ref[...] = (cw_ref[0:1, :] * xm2 + cw_ref[1:2, :] * xm1 + cw_ref[2:3, :] * x
                   + cw_ref[3:4, :] * xp1 + cb_ref[...])

    row8 = lax.broadcasted_iota(jnp.int32, (LRU_TC, C), 0) % SUBLANE

    for direction in range(2):
        reverse = direction == 1
        lam = lam_ref[direction:direction + 1, :]
        neg_c_sp = -LRU_C * (jnp.maximum(-lam, 0.0) + jnp.log1p(jnp.exp(-jnp.abs(lam))))

        def chunk_body(ci, carry):
            t0 = pl.multiple_of(ci * LRU_TC, LRU_TC)
            xc = xc_ref[pl.ds(t0, LRU_TC), :]
            z = jnp.dot(xc.astype(BF16), w_ref[0, :, direction * 2 * C:(direction + 1) * 2 * C],
                        preferred_element_type=F32) + b_ref[0, :, direction * 2 * C:(direction + 1) * 2 * C]
            log_a = neg_c_sp * jax.nn.sigmoid(z[:, :C])
            a = jnp.exp(log_a)
            th = jnp.tanh(log_a)
            mult = jnp.sqrt(-2.0 * th / (1.0 - th))
            u = mult * jax.nn.sigmoid(z[:, C:]) * xc
            for d in (1, 2, 4):
                if reverse:
                    ar = pltpu.roll(a, LRU_TC - d, 0)
                    ur = pltpu.roll(u, LRU_TC - d, 0)
                    m = row8 < SUBLANE - d
                else:
                    ar = pltpu.roll(a, d, 0)
                    ur = pltpu.roll(u, d, 0)
                    m = row8 >= d
                u = jnp.where(m, a * ur + u, u)
                a = jnp.where(m, a * ar, a)
            j0 = pl.multiple_of(ci * tiles_per_chunk, tiles_per_chunk)
            a3[pl.ds(j0, tiles_per_chunk)] = a.reshape(tiles_per_chunk, SUBLANE, C)
            u3[pl.ds(j0, tiles_per_chunk)] = u.reshape(tiles_per_chunk, SUBLANE, C)
            return carry

        lax.fori_loop(0, S // LRU_TC, chunk_body, 0)

        edge = 0 if reverse else SUBLANE - 1

        def carry_body(jj, c):
            j = (n_tiles - 1 - jj) if reverse else jj
            car_ref[j] = c
            return u3[j, edge:edge + 1, :] + a3[j, edge:edge + 1, :] * c

        lax.fori_loop(0, n_tiles, carry_body, jnp.zeros((1, C), F32))

        hdir = u3[...] + a3[...] * car_ref[...]
        if reverse:
            h_ref[...] = h_ref[...] + hdir
        else:
            h_ref[...] = hdir

    hsum = h_ref[...].reshape(S, C)
    o_ref[0] = (hsum * _silu(g_ref[0])).astype(o_ref.dtype)


def _lru_weights(w_a, b_a, w_x, b_x):
    n_grp = LRU_WIDTH // LRU_CH
    bpg = LRU_CH // LRU_BLOCK

    def dense(w):
        w = w.reshape(n_grp, bpg, LRU_BLOCK, LRU_BLOCK)
        eye = jnp.eye(bpg, dtype=w.dtype)
        return jnp.einsum('gncd,nm->gncmd', w, eye).reshape(n_grp, LRU_CH, LRU_CH)

    w = jnp.concatenate([dense(w_a[0]), dense(w_x[0]), dense(w_a[1]), dense(w_x[1])], axis=-1)
    grp = lambda b: b.reshape(n_grp, 1, LRU_CH)
    b = jnp.concatenate([grp(b_a[0]), grp(b_x[0]), grp(b_a[1]), grp(b_x[1])], axis=-1)
    return w.astype(BF16), b.astype(F32)


def _lru(rest, conv_w, conv_b, w_a, b_a, w_x, b_x, lru_lambda):
    w, b = _lru_weights(w_a, b_a, w_x, b_x)
    n_grp = LRU_WIDTH // LRU_CH
    xb0 = LRU_WIDTH // LRU_CH
    gb0 = 2 * LRU_WIDTH // LRU_CH
    return pl.pallas_call(
        _lru_kernel,
        grid=(BATCH, n_grp),
        in_specs=[
            pl.BlockSpec((1, SEQ, LRU_CH), lambda bb, c: (bb, 0, xb0 + c)),
            pl.BlockSpec((1, SEQ, LRU_CH), lambda bb, c: (bb, 0, gb0 + c)),
            pl.BlockSpec((4, LRU_CH), lambda bb, c: (0, c)),
            pl.BlockSpec((1, LRU_CH), lambda bb, c: (0, c)),
            pl.BlockSpec((1, LRU_CH, 4 * LRU_CH), lambda bb, c: (c, 0, 0)),
            pl.BlockSpec((1, 1, 4 * LRU_CH), lambda bb, c: (c, 0, 0)),
            pl.BlockSpec((2, LRU_CH), lambda bb, c: (0, c)),
        ],
        out_specs=pl.BlockSpec((1, SEQ, LRU_CH), lambda bb, c: (bb, 0, c)),
        out_shape=jax.ShapeDtypeStruct((BATCH, SEQ, LRU_WIDTH), BF16),
        scratch_shapes=[
            pltpu.VMEM((SEQ, LRU_CH), F32),
            pltpu.VMEM((SEQ // SUBLANE, SUBLANE, LRU_CH), F32),
            pltpu.VMEM((SEQ // SUBLANE, SUBLANE, LRU_CH), F32),
            pltpu.VMEM((SEQ // SUBLANE, 1, LRU_CH), F32),
            pltpu.VMEM((SEQ // SUBLANE, SUBLANE, LRU_CH), F32),
        ],
        compiler_params=_cparams(),
        name="rglru",
    )(rest, rest, conv_w, conv_b.reshape(1, LRU_WIDTH), w, b, lru_lambda)


def _na_rpb_kernel(rpb_ref, dc_ref, o_ref):
    dc = dc_ref[...]
    n_dc = 2 * NA_WIN_C - 1
    val = jnp.broadcast_to(rpb_ref[:, 0:1], o_ref.shape)
    for j in range(1, n_dc):
        val = jnp.where(dc == j, rpb_ref[:, j:j + 1], val)
    o_ref[...] = val


def _na_bias(rpb):
    n_dr = 2 * NA_WIN_R - 1
    n_dc = 2 * NA_WIN_C - 1
    cols = np.arange(GRID_W)
    dc_idx = (np.clip(cols[None, :] - cols[:, None], -(NA_WIN_C - 1), NA_WIN_C - 1)
              + NA_WIN_C - 1).astype(np.int32)
    toe = pl.pallas_call(
        _na_rpb_kernel,
        out_shape=jax.ShapeDtypeStruct((NA_HEADS * n_dr, GRID_W * GRID_W), F32),
        compiler_params=_cparams(),
        name="na_rpb",
    )(rpb.reshape(NA_HEADS * n_dr, n_dc), jnp.asarray(dc_idx.reshape(1, -1)))
    toe = toe.reshape(NA_HEADS, n_dr, GRID_W, GRID_W)
    col_start = np.clip(cols - NA_WIN_C // 2, 0, GRID_W - NA_WIN_C)
    col_mask = (cols[None, :] >= col_start[:, None]) & (cols[None, :] < col_start[:, None] + NA_WIN_C)
    neg = jnp.asarray(np.where(col_mask, 0.0, -np.inf).astype(np.float32))
    per_cls = []
    for cls in range(NA_WIN_R):
        blk = toe[:, NA_WIN_R - 1 - cls:2 * NA_WIN_R - 1 - cls] + neg[None, None]
        per_cls.append(blk.transpose(0, 2, 1, 3).reshape(NA_HEADS, GRID_W, NA_WIN_R * GRID_W))
    bias = jnp.stack(per_cls, axis=1)
    bias = bias.reshape(NA_HEADS // 2, 2, NA_WIN_R, GRID_W, NA_WIN_R * GRID_W)
    return bias.transpose(0, 2, 1, 3, 4).reshape(NA_HEADS // 2, NA_WIN_R, 2 * GRID_W, NA_WIN_R * GRID_W)


def _na_kernel(q_ref, k_ref, v_ref, g_ref, bias_ref, o_ref):
    win = NA_WIN_R * GRID_W
    scale = jnp.asarray(NA_HEAD_DIM ** -0.5, BF16)
    lane = lax.broadcasted_iota(jnp.int32, (GRID_W, LANE), 1)
    lo = lane < NA_HEAD_DIM

    def group_body(gi, carry):
        q0s, vws, ss = [], [], []
        for i in range(NA_ROWS_PER_STEP):
            r = gi * NA_ROWS_PER_STEP + i
            rs = jnp.clip(r - NA_WIN_R // 2, 0, GRID_ROWS - NA_WIN_R)
            k0 = pl.multiple_of(rs * GRID_W, GRID_W)
            q0 = pl.multiple_of(r * GRID_W, GRID_W)
            q = q_ref[0, pl.ds(q0, GRID_W), :] * scale
            zero = jnp.zeros_like(q)
            q2 = jnp.concatenate([jnp.where(lo, q, zero), jnp.where(lo, zero, q)], axis=0)
            kw = k_ref[0, pl.ds(k0, win), :]
            ss.append(lax.dot_general(q2, kw, NT_DIMS, preferred_element_type=F32) + bias_ref[0, r - rs])
            vws.append(v_ref[0, pl.ds(k0, win), :])
            q0s.append(q0)
        es, rinv = [], []
        for s in ss:
            e = jnp.exp(s - jnp.max(s, axis=-1, keepdims=True))
            rinv.append(1.0 / jnp.sum(e, axis=-1, keepdims=True))
            es.append(e.astype(BF16))
        pvs = [jnp.dot(e, vw, preferred_element_type=F32) for e, vw in zip(es, vws)]
        for pv, ri, q0 in zip(pvs, rinv, q0s):
            o2 = pv * ri
            o = jnp.where(lo, o2[:GRID_W], o2[GRID_W:])
            o_ref[0, pl.ds(q0, GRID_W), :] = (o * _silu(g_ref[0, pl.ds(q0, GRID_W), :])).astype(o_ref.dtype)
        return carry

    lax.fori_loop(0, GRID_ROWS // NA_ROWS_PER_STEP, group_body, 0)


def _na(qkv, gate, bias):
    n_pair = NA_HEADS // 2
    return pl.pallas_call(
        _na_kernel,
        grid=(BATCH, n_pair),
        in_specs=[
            pl.BlockSpec((1, SEQ, LANE), lambda b, p: (b, 0, p)),
            pl.BlockSpec((1, SEQ, LANE), lambda b, p: (b, 0, n_pair + p)),
            pl.BlockSpec((1, SEQ, LANE), lambda b, p: (b, 0, 2 * n_pair + p)),
            pl.BlockSpec((1, SEQ, LANE), lambda b, p: (b, 0, p)),
            pl.BlockSpec((1, NA_WIN_R, 2 * GRID_W, NA_WIN_R * GRID_W), lambda b, p: (p, 0, 0, 0)),
        ],
        out_specs=pl.BlockSpec((1, SEQ, LANE), lambda b, p: (b, 0, p)),
        out_shape=jax.ShapeDtypeStruct((BATCH, SEQ, NA_WIDTH), BF16),
        compiler_params=_cparams(),
        name="nbr_attn",
    )(qkv, qkv, qkv, gate, bias)


def kernel(x, c, ada_w, ada_b, norm_g, final_g, t5_table, even_w_in, even_w_out, da_lam, da_subln_g,
           lru_conv_w, lru_conv_b, lru_w_a, lru_b_a, lru_w_x, lru_b_x, lru_lambda,
           odd_w_in, odd_w_out, na_rpb):
    mod = _ada_mod(c, ada_w, ada_b)
    tiles = _t5_tiles(t5_table)
    for l in range(DEPTH):
        mod3 = mod[l].reshape(BATCH, 3, D_MODEL)
        last = final_g if l == DEPTH - 1 else None
        if l % 2 == 0:
            e = l // 2
            lam_init = 0.8 - 0.6 * math.exp(-0.3 * l)
            qkv, rest = _inproj(x, mod3, norm_g[l], even_w_in[e].astype(BF16), 3 * DA_WIDTH,
                                DA_WIDTH + 2 * LRU_WIDTH)
            mix_a = _dattn(qkv, rest, tiles, da_lam[e], da_subln_g[e], lam_init)
            mix_b = _lru(rest, lru_conv_w[e], lru_conv_b[e], lru_w_a[e], lru_b_a[e], lru_w_x[e],
                         lru_b_x[e], lru_lambda[e])
            x = _outproj([mix_a, mix_b], even_w_out[e].astype(BF16), x, mod3, last)
        else:
            o = l // 2
            qkv, gate = _inproj(x, mod3, norm_g[l], odd_w_in[o].astype(BF16), 3 * NA_WIDTH, NA_WIDTH)
            mix = _na(qkv, gate, _na_bias(na_rpb[o]))
            x = _outproj([mix], odd_w_out[o].astype(BF16), x, mod3, last)
    return x
```

```python
import functools
import math

import numpy as np
import jax
import jax.numpy as jnp
from jax import lax
from jax.experimental import pallas as pl
from jax.experimental.pallas import tpu as pltpu

F32 = jnp.float32
BF16 = jnp.bfloat16

D_MODEL = 1024
BATCH = 16
SEQ = 2048
DEPTH = 4
GRID_W = 64
EPS = 1e-6

DA_HEADS = 4
DA_HEAD_DIM = 64
DA_WIDTH = DA_HEADS * 2 * DA_HEAD_DIM
T5_BUCKETS = 32
T5_MAX_DIST = 128
LRU_WIDTH = D_MODEL // 2
LRU_BLOCKS = 8
LRU_BLOCK = LRU_WIDTH // LRU_BLOCKS
LRU_C = 8.0
NA_HEADS = 16
NA_HEAD_DIM = 64
NA_WIDTH = NA_HEADS * NA_HEAD_DIM
NA_WIN_R = 8
NA_WIN_C = 16
GRID_ROWS = SEQ // GRID_W

LANE = 128
SUBLANE = 8
VMEM_LIMIT = 48 * 1024 * 1024

TM_PROJ = 512
TQ_ATTN = 256
LRU_CH = 256
LRU_TC = 256
N_TILE_DIAG = 2 * (SEQ // LANE) - 1
DA_STRIP = SUBLANE
LOG2E = math.log2(math.e)
NA_ROWS_PER_STEP = 8

NT_DIMS = (((1,), (1,)), ((), ()))


def _cparams():
    return pltpu.CompilerParams(vmem_limit_bytes=VMEM_LIMIT)


def _silu(x):
    return x * jax.nn.sigmoid(x)


def _ada_kernel(c_ref, w_ref, b_ref, o_ref):
    ca = _silu(c_ref[...])
    o_ref[0] = jnp.dot(ca, w_ref[0], preferred_element_type=F32) + b_ref[0]


def _ada_mod(c, ada_w, ada_b):
    tn = 1536
    n = 3 * D_MODEL
    return pl.pallas_call(
        _ada_kernel,
        grid=(DEPTH, n // tn),
        in_specs=[
            pl.BlockSpec((BATCH, D_MODEL), lambda l, j: (0, 0)),
            pl.BlockSpec((1, D_MODEL, tn), lambda l, j: (l, 0, j)),
            pl.BlockSpec((1, 1, tn), lambda l, j: (l, 0, j)),
        ],
        out_specs=pl.BlockSpec((1, BATCH, tn), lambda l, j: (l, 0, j)),
        out_shape=jax.ShapeDtypeStruct((DEPTH, BATCH, n), F32),
        compiler_params=_cparams(),
        name="ada_mod",
    )(c, ada_w, ada_b.reshape(DEPTH, 1, n))


def _inproj_kernel(x_ref, mod_ref, g_ref, w_ref, o1_ref, o2_ref, *, n1, n2, chunk):
    x = x_ref[0]
    ms = jnp.mean(x * x, axis=-1, keepdims=True)
    y = x * lax.rsqrt(ms + EPS) * g_ref[...]
    h = (y * (1.0 + mod_ref[0, 1:2, :]) + mod_ref[0, 0:1, :]).astype(BF16)
    for j in range(0, n1, chunk):
        o1_ref[0, :, j:j + chunk] = jnp.dot(
            h, w_ref[:, j:j + chunk], preferred_element_type=F32).astype(o1_ref.dtype)
    for j in range(0, n2, chunk):
        o2_ref[0, :, j:j + chunk] = jnp.dot(
            h, w_ref[:, n1 + j:n1 + j + chunk], preferred_element_type=F32).astype(o2_ref.dtype)


def _inproj(x, mod3, g, w_bf16, n1, n2):
    kern = functools.partial(_inproj_kernel, n1=n1, n2=n2, chunk=512)
    return pl.pallas_call(
        kern,
        grid=(BATCH, SEQ // TM_PROJ),
        in_specs=[
            pl.BlockSpec((1, TM_PROJ, D_MODEL), lambda b, i: (b, i, 0)),
            pl.BlockSpec((1, 3, D_MODEL), lambda b, i: (b, 0, 0)),
            pl.BlockSpec((1, D_MODEL), lambda b, i: (0, 0)),
            pl.BlockSpec((D_MODEL, n1 + n2), lambda b, i: (0, 0)),
        ],
        out_specs=[
            pl.BlockSpec((1, TM_PROJ, n1), lambda b, i: (b, i, 0)),
            pl.BlockSpec((1, TM_PROJ, n2), lambda b, i: (b, i, 0)),
        ],
        out_shape=[
            jax.ShapeDtypeStruct((BATCH, SEQ, n1), BF16),
            jax.ShapeDtypeStruct((BATCH, SEQ, n2), F32),
        ],
        compiler_params=_cparams(),
        name="inproj",
    )(x, mod3, g.reshape(1, D_MODEL), w_bf16)


def _outproj_kernel(*refs, widths, final):
    n_in = len(widths)
    m_refs = refs[:n_in]
    w_ref, x_ref, mod_ref = refs[n_in:n_in + 3]
    rest = refs[n_in + 3:]
    if final:
        fg_ref, o_ref = rest
    else:
        (o_ref,) = rest
    y = None
    off = 0
    for m_ref, wd in zip(m_refs, widths):
        part = jnp.dot(m_ref[0], w_ref[off:off + wd, :], preferred_element_type=F32)
        y = part if y is None else y + part
        off += wd
    xn = x_ref[0] + mod_ref[0, 2:3, :] * y
    if final:
        ms = jnp.mean(xn * xn, axis=-1, keepdims=True)
        xn = xn * lax.rsqrt(ms + EPS) * fg_ref[...]
    o_ref[0] = xn


def _outproj(mixed, w_bf16, x, mod3, final_g=None):
    widths = tuple(m.shape[-1] for m in mixed)
    final = final_g is not None
    kern = functools.partial(_outproj_kernel, widths=widths, final=final)
    in_specs = [pl.BlockSpec((1, TM_PROJ, wd), lambda b, i: (b, i, 0)) for wd in widths]
    in_specs += [
        pl.BlockSpec((sum(widths), D_MODEL), lambda b, i: (0, 0)),
        pl.BlockSpec((1, TM_PROJ, D_MODEL), lambda b, i: (b, i, 0)),
        pl.BlockSpec((1, 3, D_MODEL), lambda b, i: (b, 0, 0)),
    ]
    args = list(mixed) + [w_bf16, x, mod3]
    if final:
        in_specs.append(pl.BlockSpec((1, D_MODEL), lambda b, i: (0, 0)))
        args.append(final_g.reshape(1, D_MODEL))
    return pl.pallas_call(
        kern,
        grid=(BATCH, SEQ // TM_PROJ),
        in_specs=in_specs,
        out_specs=pl.BlockSpec((1, TM_PROJ, D_MODEL), lambda b, i: (b, i, 0)),
        out_shape=jax.ShapeDtypeStruct((BATCH, SEQ, D_MODEL), F32),
        compiler_params=_cparams(),
        name="outproj_final" if final else "outproj",
    )(*args)


def _t5_bucket_np(rel):
    nb = T5_BUCKETS // 2
    max_exact = nb // 2
    ret = np.where(rel > 0, nb, 0)
    n = np.abs(rel)
    nf = np.maximum(n, 1).astype(np.float32)
    large = max_exact + (np.log(nf / np.float32(max_exact)) / np.float32(math.log(T5_MAX_DIST / max_exact))
                         * np.float32(nb - max_exact)).astype(np.int32)
    large = np.minimum(large, nb - 1)
    return (ret + np.where(n < max_exact, n, large)).astype(np.int32)


def _t5_tile_buckets():
    d = np.arange(N_TILE_DIAG)[:, None, None] - (SEQ // LANE - 1)
    i = np.arange(LANE)[None, :, None]
    j = np.arange(LANE)[None, None, :]
    idx = _t5_bucket_np(d * LANE + j - i)
    mid = SEQ // LANE - 1
    assert (idx[:mid - 1] == T5_BUCKETS // 2 - 1).all() and (idx[mid + 2:] == T5_BUCKETS - 1).all()
    return idx


def _t5_tiles_kernel(tab_ref, idx_ref, o_ref):
    h = pl.program_id(0)

    def body(d, carry):
        idx = idx_ref[d]
        val = jnp.full(idx.shape, tab_ref[h, 0], F32)
        for j in range(1, T5_BUCKETS):
            val = jnp.where(idx == j, tab_ref[h, j], val)
        o_ref[0, d] = val
        return carry

    lax.fori_loop(0, N_TILE_DIAG, body, 0)


def _t5_tiles(t5_table):
    idx = jnp.asarray(_t5_tile_buckets())
    return pl.pallas_call(
        _t5_tiles_kernel,
        grid=(DA_HEADS,),
        in_specs=[
            pl.BlockSpec(memory_space=pltpu.SMEM),
            pl.BlockSpec((N_TILE_DIAG, LANE, LANE), lambda h: (0, 0, 0)),
        ],
        out_specs=pl.BlockSpec((1, N_TILE_DIAG, LANE, LANE), lambda h: (h, 0, 0, 0)),
        out_shape=jax.ShapeDtypeStruct((DA_HEADS, N_TILE_DIAG, LANE, LANE), F32),
        compiler_params=_cparams(),
        name="t5_tiles",
    )(t5_table, idx)


def _dattn_kernel(tab_ref, lam_ref, q_ref, k_ref, v_ref, t_ref, ga_ref, sg_ref, o_ref,
                  s_ref, a_ref, r_ref, *, lam_init):
    h = pl.program_id(0)
    n_kb = SEQ // LANE
    n_tiles = SEQ // TQ_ATTN
    near = 1
    c_left = tab_ref[h, T5_BUCKETS // 2 - 1]
    c_right = tab_ref[h, T5_BUCKETS - 1]

    lp = lam_ref[...]
    lam = (jnp.exp(jnp.sum(lp[0:1] * lp[1:2], axis=-1, keepdims=True))
           - jnp.exp(jnp.sum(lp[2:3] * lp[3:4], axis=-1, keepdims=True)) + lam_init)
    lane = lax.broadcasted_iota(jnp.int32, (TQ_ATTN, LANE), 1)

    def phase_a(i, slot):
        q = q_ref[0, i * TQ_ATTN:(i + 1) * TQ_ATTN, :]
        zero = jnp.zeros_like(q)
        qst = jnp.concatenate([jnp.where(lane < DA_HEAD_DIM, q, zero),
                               jnp.where(lane >= DA_HEAD_DIM, q, zero)], axis=0)
        s_ref[slot] = lax.dot_general(qst, k_ref[0], NT_DIMS, preferred_element_type=F32)

    def strip_softmax(slot, row0, qb, rr):
        cols, m_left, m_right, m_near = [], None, None, None
        for kb in range(n_kb):
            x = s_ref[slot, row0:row0 + DA_STRIP, kb * LANE:(kb + 1) * LANE]
            d = kb - qb
            if abs(d) <= near:
                x = x + t_ref[0, d + n_kb - 1, rr:rr + DA_STRIP, :]
                m_near = x if m_near is None else jnp.maximum(m_near, x)
            elif d < 0:
                m_left = x if m_left is None else jnp.maximum(m_left, x)
            else:
                m_right = x if m_right is None else jnp.maximum(m_right, x)
            cols.append(x)
        m = m_near
        if m_left is not None:
            m = jnp.maximum(m, m_left + c_left)
        if m_right is not None:
            m = jnp.maximum(m, m_right + c_right)
        m = jnp.max(m, axis=-1, keepdims=True)
        off_left, off_right = m - c_left, m - c_right
        es, tot = [], None
        for kb in range(n_kb):
            d = kb - qb
            off = m if abs(d) <= near else (off_left if d < 0 else off_right)
            e = jnp.exp2(cols[kb] - off)
            tot = e if tot is None else tot + e
            es.append(e)
        return es, jnp.sum(tot, axis=-1, keepdims=True)

    def strip_weights(i, slot, r0):
        qb = (i * TQ_ATTN + r0) // LANE
        rr = (i * TQ_ATTN + r0) % LANE
        e1, l1 = strip_softmax(slot, r0, qb, rr)
        e2, l2 = strip_softmax(slot, TQ_ATTN + r0, qb, rr)
        cc = lam * l1 * (1.0 / l2)
        r_ref[slot, r0:r0 + DA_STRIP, :] = jnp.broadcast_to(1.0 / l1, (DA_STRIP, LANE))
        return [x1 - cc * x2 for x1, x2 in zip(e1, e2)]

    def phase_b(i, slot):
        for r0 in range(0, TQ_ATTN, 2 * DA_STRIP):
            lo = strip_weights(i, slot, r0)
            hi = strip_weights(i, slot, r0 + DA_STRIP)
            for kb in range(n_kb):
                a_ref[slot, r0:r0 + 2 * DA_STRIP, kb * LANE:(kb + 1) * LANE] = (
                    jnp.concatenate([lo[kb], hi[kb]], axis=0).astype(BF16))

    def phase_c(i, slot):
        o = jnp.dot(a_ref[slot], v_ref[0], preferred_element_type=F32) * r_ref[slot]
        ms = jnp.mean(o * o, axis=-1, keepdims=True)
        o = o * lax.rsqrt(ms + EPS) * sg_ref[...] * (1.0 - lam_init)
        rows = slice(i * TQ_ATTN, (i + 1) * TQ_ATTN)
        o_ref[0, rows, :] = (o * _silu(ga_ref[0, rows, :])).astype(o_ref.dtype)

    phase_a(0, 0)
    for i in range(n_tiles):
        if i + 1 < n_tiles:
            phase_a(i + 1, (i + 1) % 2)
        phase_b(i, i % 2)
        phase_c(i, i % 2)


def _dattn(qkv, rest, tiles, tab_log2, lam_p, subln_g, lam_init):
    kern = functools.partial(_dattn_kernel, lam_init=lam_init)
    nh = DA_HEADS
    return pl.pallas_call(
        kern,
        grid=(nh, BATCH),
        in_specs=[
            pl.BlockSpec(memory_space=pltpu.SMEM),
            pl.BlockSpec((4, DA_HEAD_DIM), lambda h, b: (0, 0)),
            pl.BlockSpec((1, SEQ, LANE), lambda h, b: (b, 0, h)),
            pl.BlockSpec((1, SEQ, LANE), lambda h, b: (b, 0, nh + h)),
            pl.BlockSpec((1, SEQ, LANE), lambda h, b: (b, 0, 2 * nh + h)),
            pl.BlockSpec((1, N_TILE_DIAG, LANE, LANE), lambda h, b: (h, 0, 0, 0)),
            pl.BlockSpec((1, SEQ, LANE), lambda h, b: (b, 0, h)),
            pl.BlockSpec((1, LANE), lambda h, b: (0, 0)),
        ],
        out_specs=pl.BlockSpec((1, SEQ, LANE), lambda h, b: (b, 0, h)),
        out_shape=jax.ShapeDtypeStruct((BATCH, SEQ, DA_WIDTH), BF16),
        scratch_shapes=[
            pltpu.VMEM((2, 2 * TQ_ATTN, SEQ), F32),
            pltpu.VMEM((2, TQ_ATTN, SEQ), BF16),
            pltpu.VMEM((2, TQ_ATTN, LANE), F32),
        ],
        compiler_params=_cparams(),
        name="diff_attn",
    )(tab_log2, lam_p, qkv, qkv, qkv, tiles, rest, subln_g.reshape(1, LANE))


def _lru_kernel(x_ref, g_ref, cw_ref, cb_ref, w_ref, b_ref, lam_ref, o_ref,
                xc_ref, a3, u3, car_ref):
    S, C = SEQ, LRU_CH
    n_tiles = S // SUBLANE
    tiles_per_chunk = LRU_TC // SUBLANE

    x = x_ref[0]
    row = lax.broadcasted_iota(jnp.int32, (S, C), 0)
    xm2 = jnp.where(row >= 2, pltpu.roll(x, 2, 0), 0.0)
    xm1 = jnp.where(row >= 1, pltpu.roll(x, 1, 0), 0.0)
    xp1 = jnp.where(row < S - 1, pltpu.roll(x, S - 1, 0), 0.0)
    xc_ref[...] = (cw_ref[0:1, :] * xm2 + cw_ref[1:2, :] * xm1 + cw_ref[2:3, :] * x
                   + cw_ref[3:4, :] * xp1 + cb_ref[...])

    row8 = lax.broadcasted_iota(jnp.int32, (tiles_per_chunk, SUBLANE, C), 1)
    lam = lam_ref[...]
    half_c = (-0.5 * LRU_C) * (jnp.maximum(-lam, 0.0) + jnp.log1p(jnp.exp(-jnp.abs(lam))))

    def chunk_body(ci, carry):
        t0 = pl.multiple_of(ci * LRU_TC, LRU_TC)
        j0 = pl.multiple_of(ci * tiles_per_chunk, tiles_per_chunk)
        xc = xc_ref[pl.ds(t0, LRU_TC), :]
        xch = 0.5 * xc
        zh = jnp.dot(xc.astype(BF16), w_ref[0], preferred_element_type=F32) + b_ref[0]
        for direction in range(2):
            hc = half_c[direction:direction + 1, :]
            log_a = hc * jnp.tanh(zh[:, 2 * direction * C:(2 * direction + 1) * C]) + hc
            a = jnp.exp(log_a)
            mult = jnp.sqrt(-jnp.tanh(log_a) * (a * a + 1.0))
            u = mult * (jnp.tanh(zh[:, (2 * direction + 1) * C:(2 * direction + 2) * C]) + 1.0) * xch
            a = a.reshape(tiles_per_chunk, SUBLANE, C)
            u = u.reshape(tiles_per_chunk, SUBLANE, C)
            for d in (1, 2, 4):
                if direction == 1:
                    ar = pltpu.roll(a, SUBLANE - d, 1)
                    ur = pltpu.roll(u, SUBLANE - d, 1)
                    m = row8 < SUBLANE - d
                else:
                    ar = pltpu.roll(a, d, 1)
                    ur = pltpu.roll(u, d, 1)
                    m = row8 >= d
                u = jnp.where(m, a * ur + u, u)
                a = jnp.where(m, a * ar, a)
            a3[direction, pl.ds(j0, tiles_per_chunk)] = a
            u3[direction, pl.ds(j0, tiles_per_chunk)] = u
        return carry

    lax.fori_loop(0, S // LRU_TC, chunk_body, 0)

    def carry_body(jf, c):
        cf, cb = c
        jb = n_tiles - 1 - jf
        car_ref[0, jf] = cf
        car_ref[1, jb] = cb
        last = SUBLANE - 1
        return (u3[0, jf, last:, :] + a3[0, jf, last:, :] * cf,
                u3[1, jb, 0:1, :] + a3[1, jb, 0:1, :] * cb)

    zero = jnp.zeros((1, C), F32)
    lax.fori_loop(0, n_tiles, carry_body, (zero, zero), unroll=8)

    hsum = (u3[0] + a3[0] * car_ref[0]) + (u3[1] + a3[1] * car_ref[1])
    o_ref[0] = (hsum.reshape(S, C) * _silu(g_ref[0])).astype(o_ref.dtype)


def _lru_weights(w_a, b_a, w_x, b_x):
    n_grp = LRU_WIDTH // LRU_CH
    bpg = LRU_CH // LRU_BLOCK

    def dense(w):
        w = w.reshape(n_grp, bpg, LRU_BLOCK, LRU_BLOCK)
        eye = jnp.eye(bpg, dtype=w.dtype)
        return jnp.einsum('gncd,nm->gncmd', w, eye).reshape(n_grp, LRU_CH, LRU_CH)

    w = jnp.concatenate([dense(w_a[0]), dense(w_x[0]), dense(w_a[1]), dense(w_x[1])], axis=-1)
    grp = lambda b: b.reshape(n_grp, 1, LRU_CH)
    b = jnp.concatenate([grp(b_a[0]), grp(b_x[0]), grp(b_a[1]), grp(b_x[1])], axis=-1)
    return (0.5 * w).astype(BF16), (0.5 * b).astype(F32)


def _lru(rest, conv_w, conv_b, w_a, b_a, w_x, b_x, lru_lambda):
    w, b = _lru_weights(w_a, b_a, w_x, b_x)
    n_grp = LRU_WIDTH // LRU_CH
    xb0 = LRU_WIDTH // LRU_CH
    gb0 = 2 * LRU_WIDTH // LRU_CH
    return pl.pallas_call(
        _lru_kernel,
        grid=(BATCH, n_grp),
        in_specs=[
            pl.BlockSpec((1, SEQ, LRU_CH), lambda bb, c: (bb, 0, xb0 + c)),
            pl.BlockSpec((1, SEQ, LRU_CH), lambda bb, c: (bb, 0, gb0 + c)),
            pl.BlockSpec((4, LRU_CH), lambda bb, c: (0, c)),
            pl.BlockSpec((1, LRU_CH), lambda bb, c: (0, c)),
            pl.BlockSpec((1, LRU_CH, 4 * LRU_CH), lambda bb, c: (c, 0, 0)),
            pl.BlockSpec((1, 1, 4 * LRU_CH), lambda bb, c: (c, 0, 0)),
            pl.BlockSpec((2, LRU_CH), lambda bb, c: (0, c)),
        ],
        out_specs=pl.BlockSpec((1, SEQ, LRU_CH), lambda bb, c: (bb, 0, c)),
        out_shape=jax.ShapeDtypeStruct((BATCH, SEQ, LRU_WIDTH), BF16),
        scratch_shapes=[
            pltpu.VMEM((SEQ, LRU_CH), F32),
            pltpu.VMEM((2, SEQ // SUBLANE, SUBLANE, LRU_CH), F32),
            pltpu.VMEM((2, SEQ // SUBLANE, SUBLANE, LRU_CH), F32),
            pltpu.VMEM((2, SEQ // SUBLANE, 1, LRU_CH), F32),
        ],
        compiler_params=_cparams(),
        name="rglru",
    )(rest, rest, conv_w, conv_b.reshape(1, LRU_WIDTH), w, b, lru_lambda)


def _na_rpb_kernel(rpb_ref, dc_ref, o_ref):
    dc = dc_ref[...]
    n_dc = 2 * NA_WIN_C - 1
    val = jnp.broadcast_to(rpb_ref[:, 0:1], o_ref.shape)
    for j in range(1, n_dc):
        val = jnp.where(dc == j, rpb_ref[:, j:j + 1], val)
    o_ref[...] = val


def _na_bias(rpb):
    n_dr = 2 * NA_WIN_R - 1
    n_dc = 2 * NA_WIN_C - 1
    cols = np.arange(GRID_W)
    dc_idx = (np.clip(cols[None, :] - cols[:, None], -(NA_WIN_C - 1), NA_WIN_C - 1)
              + NA_WIN_C - 1).astype(np.int32)
    toe = pl.pallas_call(
        _na_rpb_kernel,
        out_shape=jax.ShapeDtypeStruct((NA_HEADS * n_dr, GRID_W * GRID_W), F32),
        compiler_params=_cparams(),
        name="na_rpb",
    )(rpb.reshape(NA_HEADS * n_dr, n_dc), jnp.asarray(dc_idx.reshape(1, -1)))
    toe = LOG2E * toe.reshape(NA_HEADS, n_dr, GRID_W, GRID_W)
    col_start = np.clip(cols - NA_WIN_C // 2, 0, GRID_W - NA_WIN_C)
    col_mask = (cols[None, :] >= col_start[:, None]) & (cols[None, :] < col_start[:, None] + NA_WIN_C)
    neg = jnp.asarray(np.where(col_mask, 0.0, -np.inf).astype(np.float32))
    per_cls = []
    for cls in range(NA_WIN_R):
        blk = toe[:, NA_WIN_R - 1 - cls:2 * NA_WIN_R - 1 - cls] + neg[None, None]
        per_cls.append(blk.transpose(0, 2, 1, 3).reshape(NA_HEADS, GRID_W, NA_WIN_R * GRID_W))
    bias = jnp.stack(per_cls, axis=1)
    bias = bias.reshape(NA_HEADS // 2, 2, NA_WIN_R, GRID_W, NA_WIN_R * GRID_W)
    return bias.transpose(0, 2, 1, 3, 4).reshape(NA_HEADS // 2, NA_WIN_R, 2 * GRID_W, NA_WIN_R * GRID_W)


def _na_kernel(q_ref, k_ref, v_ref, g_ref, bias_ref, o_ref):
    win = NA_WIN_R * GRID_W
    lane = lax.broadcasted_iota(jnp.int32, (GRID_W, LANE), 1)
    lo = lane < NA_HEAD_DIM

    def group_body(gi, carry):
        q0s, vws, ss = [], [], []
        for i in range(NA_ROWS_PER_STEP):
            r = gi * NA_ROWS_PER_STEP + i
            rs = jnp.clip(r - NA_WIN_R // 2, 0, GRID_ROWS - NA_WIN_R)
            k0 = pl.multiple_of(rs * GRID_W, GRID_W)
            q0 = pl.multiple_of(r * GRID_W, GRID_W)
            q = q_ref[0, pl.ds(q0, GRID_W), :]
            zero = jnp.zeros_like(q)
            q2 = jnp.concatenate([jnp.where(lo, q, zero), jnp.where(lo, zero, q)], axis=0)
            kw = k_ref[0, pl.ds(k0, win), :]
            ss.append(lax.dot_general(q2, kw, NT_DIMS, preferred_element_type=F32) + bias_ref[0, r - rs])
            vws.append(v_ref[0, pl.ds(k0, win), :])
            q0s.append(q0)
        es, rinv = [], []
        for s in ss:
            e = jnp.exp2(s - jnp.max(s, axis=-1, keepdims=True))
            rinv.append(1.0 / jnp.sum(e, axis=-1, keepdims=True))
            es.append(e.astype(BF16))
        pvs = [jnp.dot(e, vw, preferred_element_type=F32) for e, vw in zip(es, vws)]
        for pv, ri, q0 in zip(pvs, rinv, q0s):
            o2 = pv * ri
            o = jnp.where(lo, o2[:GRID_W], o2[GRID_W:])
            o_ref[0, pl.ds(q0, GRID_W), :] = (o * _silu(g_ref[0, pl.ds(q0, GRID_W), :])).astype(o_ref.dtype)
        return carry

    lax.fori_loop(0, GRID_ROWS // NA_ROWS_PER_STEP, group_body, 0)


def _na(qkv, gate, bias):
    n_pair = NA_HEADS // 2
    return pl.pallas_call(
        _na_kernel,
        grid=(BATCH, n_pair),
        in_specs=[
            pl.BlockSpec((1, SEQ, LANE), lambda b, p: (b, 0, p)),
            pl.BlockSpec((1, SEQ, LANE), lambda b, p: (b, 0, n_pair + p)),
            pl.BlockSpec((1, SEQ, LANE), lambda b, p: (b, 0, 2 * n_pair + p)),
            pl.BlockSpec((1, SEQ, LANE), lambda b, p: (b, 0, p)),
            pl.BlockSpec((1, NA_WIN_R, 2 * GRID_W, NA_WIN_R * GRID_W), lambda b, p: (p, 0, 0, 0)),
        ],
        out_specs=pl.BlockSpec((1, SEQ, LANE), lambda b, p: (b, 0, p)),
        out_shape=jax.ShapeDtypeStruct((BATCH, SEQ, NA_WIDTH), BF16),
        compiler_params=_cparams(),
        name="nbr_attn",
    )(qkv, qkv, qkv, gate, bias)


def kernel(x, c, ada_w, ada_b, norm_g, final_g, t5_table, even_w_in, even_w_out, da_lam, da_subln_g,
           lru_conv_w, lru_conv_b, lru_w_a, lru_b_a, lru_w_x, lru_b_x, lru_lambda,
           odd_w_in, odd_w_out, na_rpb):
    mod = _ada_mod(c, ada_w, ada_b)
    tab_log2 = LOG2E * t5_table
    tiles = _t5_tiles(tab_log2)

    def w_in_bf16(w, n_q, head_dim):
        return jnp.concatenate([w[:, :n_q] * (head_dim ** -0.5 * LOG2E), w[:, n_q:]], axis=1).astype(BF16)

    for l in range(DEPTH):
        mod3 = mod[l].reshape(BATCH, 3, D_MODEL)
        last = final_g if l == DEPTH - 1 else None
        if l % 2 == 0:
            e = l // 2
            lam_init = 0.8 - 0.6 * math.exp(-0.3 * l)
            qkv, rest = _inproj(x, mod3, norm_g[l], w_in_bf16(even_w_in[e], DA_WIDTH, DA_HEAD_DIM),
                                3 * DA_WIDTH, DA_WIDTH + 2 * LRU_WIDTH)
            mix_a = _dattn(qkv, rest, tiles, tab_log2, da_lam[e], da_subln_g[e], lam_init)
            mix_b = _lru(rest, lru_conv_w[e], lru_conv_b[e], lru_w_a[e], lru_b_a[e], lru_w_x[e],
                         lru_b_x[e], lru_lambda[e])
            x = _outproj([mix_a, mix_b], even_w_out[e].astype(BF16), x, mod3, last)
        else:
            o = l // 2
            qkv, gate = _inproj(x, mod3, norm_g[l], w_in_bf16(odd_w_in[o], NA_WIDTH, NA_HEAD_DIM),
                                3 * NA_WIDTH, NA_WIDTH)
            mix = _na(qkv, gate, _na_bias(na_rpb[o]))
            x = _outproj([mix], odd_w_out[o].astype(BF16), x, mod3, last)
    return x
```

```python
import functools
import math

import numpy as np
import jax
import jax.numpy as jnp
from jax import lax
from jax.experimental import pallas as pl
from jax.experimental.pallas import tpu as pltpu

F32 = jnp.float32
BF16 = jnp.bfloat16

D_MODEL = 1024
BATCH = 16
SEQ = 2048
DEPTH = 4
GRID_W = 64
EPS = 1e-6

DA_HEADS = 4
DA_HEAD_DIM = 64
DA_WIDTH = DA_HEADS * 2 * DA_HEAD_DIM
T5_BUCKETS = 32
T5_MAX_DIST = 128
LRU_WIDTH = D_MODEL // 2
LRU_BLOCKS = 8
LRU_BLOCK = LRU_WIDTH // LRU_BLOCKS
LRU_C = 8.0
NA_HEADS = 16
NA_HEAD_DIM = 64
NA_WIDTH = NA_HEADS * NA_HEAD_DIM
NA_WIN_R = 8
NA_WIN_C = 16
GRID_ROWS = SEQ // GRID_W

LANE = 128
SUBLANE = 8
VMEM_LIMIT = 48 * 1024 * 1024

TM_PROJ = 512
TQ_ATTN = 256
LRU_CH = 256
LRU_TC = 256
N_TILE_DIAG = 2 * (SEQ // LANE) - 1
T5_NEAR = 1
DA_STRIP = SUBLANE
LOG2E = math.log2(math.e)
NA_ROWS_PER_STEP = 8

NT_DIMS = (((1,), (1,)), ((), ()))


def _cparams():
    return pltpu.CompilerParams(vmem_limit_bytes=VMEM_LIMIT)


def _silu(x):
    return x * jax.nn.sigmoid(x)


def _ada_kernel(c_ref, w_ref, b_ref, o_ref):
    ca = _silu(c_ref[...])
    o_ref[0, 0] = jnp.dot(ca, w_ref[0], preferred_element_type=F32) + b_ref[0, 0]


def _ada_mod(c, ada_w, ada_b):
    return pl.pallas_call(
        _ada_kernel,
        grid=(DEPTH, 3),
        in_specs=[
            pl.BlockSpec((BATCH, D_MODEL), lambda l, j: (0, 0)),
            pl.BlockSpec((1, D_MODEL, D_MODEL), lambda l, j: (l, 0, j)),
            pl.BlockSpec((1, 1, 1, D_MODEL), lambda l, j: (l, j, 0, 0)),
        ],
        out_specs=pl.BlockSpec((1, 1, BATCH, D_MODEL), lambda l, j: (l, j, 0, 0)),
        out_shape=jax.ShapeDtypeStruct((DEPTH, 3, BATCH, D_MODEL), F32),
        compiler_params=_cparams(),
        name="ada_mod",
    )(c, ada_w, ada_b.reshape(DEPTH, 3, 1, D_MODEL))


def _rms(x, g):
    return x * lax.rsqrt(jnp.mean(x * x, axis=-1, keepdims=True) + EPS) * g


def _proj_kernel(*refs, widths, has_in, final, n1, n2, chunk):
    it = iter(refs)
    b = pl.program_id(0)
    m_refs = [next(it) for _ in widths]
    if widths:
        wo_ref, mod_ref = next(it), next(it)
    x_ref = next(it)
    if has_in:
        modn_ref, g_ref, wi_ref = next(it), next(it), next(it)
    if final:
        fg_ref = next(it)
    if widths:
        xo_ref = next(it)
    if has_in:
        o1_ref, o2_ref = next(it), next(it)

    x = x_ref[0]
    if widths:
        y, off = None, 0
        for m_ref, wd in zip(m_refs, widths):
            part = jnp.dot(m_ref[0], wo_ref[0, off:off + wd, :], preferred_element_type=F32)
            y = part if y is None else y + part
            off += wd
        x = x + mod_ref[0, 2, pl.ds(b, 1), :] * y
        xo_ref[0] = _rms(x, fg_ref[...]) if final else x
    if has_in:
        y = _rms(x, g_ref[0])
        h = (y * (1.0 + modn_ref[0, 1, pl.ds(b, 1), :]) + modn_ref[0, 0, pl.ds(b, 1), :]).astype(BF16)
        for j in range(0, n1, chunk):
            o1_ref[0, :, j:j + chunk] = jnp.dot(
                h, wi_ref[0, :, j:j + chunk], preferred_element_type=F32).astype(o1_ref.dtype)
        for j in range(0, n2, chunk):
            o2_ref[0, :, j:j + chunk] = jnp.dot(
                h, wi_ref[0, :, n1 + j:n1 + j + chunk], preferred_element_type=F32).astype(o2_ref.dtype)


def _proj(x, mod, *, mixed=(), w_out=None, layer_out=None, w_in=None, norm_g=None, layer_in=None,
          idx_out=0, idx_in=0, n1=0, n2=0, final_g=None):
    widths = tuple(m.shape[-1] for m in mixed)
    has_in = w_in is not None
    final = final_g is not None
    kern = functools.partial(_proj_kernel, widths=widths, has_in=has_in, final=final, n1=n1, n2=n2, chunk=512)
    tile = lambda wd: pl.BlockSpec((1, TM_PROJ, wd), lambda b, i: (b, i, 0))
    const = lambda shape, first: pl.BlockSpec(shape, lambda b, i: (first,) + (0,) * (len(shape) - 1),
                                              pipeline_mode=pl.Buffered(1))
    mod_spec = lambda layer: const((1, 3, BATCH, D_MODEL), layer)
    in_specs, args = [tile(wd) for wd in widths], list(mixed)
    if widths:
        in_specs += [const((1, sum(widths), D_MODEL), idx_out), mod_spec(layer_out)]
        args += [w_out, mod]
    in_specs.append(tile(D_MODEL))
    args.append(x)
    if has_in:
        in_specs += [mod_spec(layer_in), const((1, 1, D_MODEL), layer_in), const((1, D_MODEL, n1 + n2), idx_in)]
        args += [mod, norm_g.reshape(DEPTH, 1, D_MODEL), w_in]
    if final:
        in_specs.append(const((1, D_MODEL), 0))
        args.append(final_g.reshape(1, D_MODEL))
    out_specs, out_shape = [], []
    if widths:
        out_specs.append(tile(D_MODEL))
        out_shape.append(jax.ShapeDtypeStruct((BATCH, SEQ, D_MODEL), F32))
    if has_in:
        out_specs += [tile(n1), tile(n2)]
        out_shape += [jax.ShapeDtypeStruct((BATCH, SEQ, n1), BF16), jax.ShapeDtypeStruct((BATCH, SEQ, n2), F32)]
    return pl.pallas_call(
        kern,
        grid=(BATCH, SEQ // TM_PROJ),
        in_specs=in_specs,
        out_specs=out_specs,
        out_shape=out_shape,
        compiler_params=_cparams(),
        name="proj_" + ("o" if widths else "") + ("i" if has_in else "") + ("f" if final else ""),
    )(*args)


def _t5_bucket_np(rel):
    nb = T5_BUCKETS // 2
    max_exact = nb // 2
    ret = np.where(rel > 0, nb, 0)
    n = np.abs(rel)
    nf = np.maximum(n, 1).astype(np.float32)
    large = max_exact + (np.log(nf / np.float32(max_exact)) / np.float32(math.log(T5_MAX_DIST / max_exact))
                         * np.float32(nb - max_exact)).astype(np.int32)
    large = np.minimum(large, nb - 1)
    return (ret + np.where(n < max_exact, n, large)).astype(np.int32)


def _t5_tile_buckets():
    d = np.arange(N_TILE_DIAG)[:, None, None] - (SEQ // LANE - 1)
    i = np.arange(LANE)[None, :, None]
    j = np.arange(LANE)[None, None, :]
    idx = _t5_bucket_np(d * LANE + j - i)
    mid = SEQ // LANE - 1
    assert (idx[:mid - T5_NEAR] == T5_BUCKETS // 2 - 1).all() and (idx[mid + T5_NEAR + 1:] == T5_BUCKETS - 1).all()
    return idx[mid - T5_NEAR:mid + T5_NEAR + 1]


def _t5_tiles_kernel(tab_ref, idx_ref, o_ref):
    h = pl.program_id(0)
    for d in range(2 * T5_NEAR + 1):
        idx = idx_ref[d]
        val = jnp.full(idx.shape, tab_ref[h, 0], F32)
        for j in range(1, T5_BUCKETS):
            val = jnp.where(idx == j, tab_ref[h, j], val)
        o_ref[0, d] = val


def _t5_tiles(t5_table):
    idx = jnp.asarray(_t5_tile_buckets())
    n_near = 2 * T5_NEAR + 1
    return pl.pallas_call(
        _t5_tiles_kernel,
        grid=(DA_HEADS,),
        in_specs=[
            pl.BlockSpec(memory_space=pltpu.SMEM),
            pl.BlockSpec((n_near, LANE, LANE), lambda h: (0, 0, 0)),
        ],
        out_specs=pl.BlockSpec((1, n_near, LANE, LANE), lambda h: (h, 0, 0, 0)),
        out_shape=jax.ShapeDtypeStruct((DA_HEADS, n_near, LANE, LANE), F32),
        compiler_params=_cparams(),
        name="t5_tiles",
    )(t5_table, idx)


def _dattn_kernel(tab_ref, lam_ref, q_ref, k_ref, v_ref, t_ref, ga_ref, sg_ref, o_ref,
                  s_ref, a_ref, r_ref, *, lam_init):
    h = pl.program_id(0)
    n_kb = SEQ // LANE
    n_tiles = SEQ // TQ_ATTN
    c_left = tab_ref[h, T5_BUCKETS // 2 - 1]
    c_right = tab_ref[h, T5_BUCKETS - 1]

    lp = lam_ref[...]
    lam = (jnp.exp(jnp.sum(lp[0:1] * lp[1:2], axis=-1, keepdims=True))
           - jnp.exp(jnp.sum(lp[2:3] * lp[3:4], axis=-1, keepdims=True)) + lam_init)
    lane = lax.broadcasted_iota(jnp.int32, (TQ_ATTN, LANE), 1)

    def phase_a(i, slot):
        q = q_ref[0, i * TQ_ATTN:(i + 1) * TQ_ATTN, :]
        zero = jnp.zeros_like(q)
        qst = jnp.concatenate([jnp.where(lane < DA_HEAD_DIM, q, zero),
                               jnp.where(lane >= DA_HEAD_DIM, q, zero)], axis=0)
        s_ref[slot] = lax.dot_general(qst, k_ref[0], NT_DIMS, preferred_element_type=F32)

    def strip_softmax(slot, row0, qb, rr):
        cols, m_left, m_right, m_near = [], None, None, None
        for kb in range(n_kb):
            x = s_ref[slot, row0:row0 + DA_STRIP, kb * LANE:(kb + 1) * LANE]
            d = kb - qb
            if abs(d) <= T5_NEAR:
                x = x + t_ref[0, d + T5_NEAR, rr:rr + DA_STRIP, :]
                m_near = x if m_near is None else jnp.maximum(m_near, x)
            elif d < 0:
                m_left = x if m_left is None else jnp.maximum(m_left, x)
            else:
                m_right = x if m_right is None else jnp.maximum(m_right, x)
            cols.append(x)
        m = m_near
        if m_left is not None:
            m = jnp.maximum(m, m_left + c_left)
        if m_right is not None:
            m = jnp.maximum(m, m_right + c_right)
        m = jnp.max(m, axis=-1, keepdims=True)
        off_left, off_right = m - c_left, m - c_right
        es, tot = [], None
        for kb in range(n_kb):
            d = kb - qb
            off = m if abs(d) <= T5_NEAR else (off_left if d < 0 else off_right)
            e = jnp.exp2(cols[kb] - off)
            tot = e if tot is None else tot + e
            es.append(e)
        return es, jnp.sum(tot, axis=-1, keepdims=True)

    def strip_weights(i, slot, r0):
        qb = (i * TQ_ATTN + r0) // LANE
        rr = (i * TQ_ATTN + r0) % LANE
        e1, l1 = strip_softmax(slot, r0, qb, rr)
        e2, l2 = strip_softmax(slot, TQ_ATTN + r0, qb, rr)
        cc = lam * l1 * (1.0 / l2)
        r_ref[slot, r0:r0 + DA_STRIP, :] = jnp.broadcast_to(1.0 / l1, (DA_STRIP, LANE))
        return [x1 - cc * x2 for x1, x2 in zip(e1, e2)]

    def phase_b(i, slot):
        for r0 in range(0, TQ_ATTN, 2 * DA_STRIP):
            lo = strip_weights(i, slot, r0)
            hi = strip_weights(i, slot, r0 + DA_STRIP)
            for kb in range(n_kb):
                a_ref[slot, r0:r0 + 2 * DA_STRIP, kb * LANE:(kb + 1) * LANE] = (
                    jnp.concatenate([lo[kb], hi[kb]], axis=0).astype(BF16))

    def phase_c(i, slot):
        o = jnp.dot(a_ref[slot], v_ref[0], preferred_element_type=F32) * r_ref[slot]
        ms = jnp.mean(o * o, axis=-1, keepdims=True)
        o = o * lax.rsqrt(ms + EPS) * sg_ref[...] * (1.0 - lam_init)
        rows = slice(i * TQ_ATTN, (i + 1) * TQ_ATTN)
        o_ref[0, rows, :] = (o * _silu(ga_ref[0, rows, :])).astype(o_ref.dtype)

    phase_a(0, 0)
    for i in range(n_tiles):
        if i + 1 < n_tiles:
            phase_a(i + 1, (i + 1) % 2)
        phase_b(i, i % 2)
        phase_c(i, i % 2)


def _dattn(qkv, rest, tiles, tab_log2, lam_p, subln_g, lam_init):
    kern = functools.partial(_dattn_kernel, lam_init=lam_init)
    nh = DA_HEADS
    return pl.pallas_call(
        kern,
        grid=(nh, BATCH),
        in_specs=[
            pl.BlockSpec(memory_space=pltpu.SMEM),
            pl.BlockSpec((4, DA_HEAD_DIM), lambda h, b: (0, 0)),
            pl.BlockSpec((1, SEQ, LANE), lambda h, b: (b, 0, h)),
            pl.BlockSpec((1, SEQ, LANE), lambda h, b: (b, 0, nh + h)),
            pl.BlockSpec((1, SEQ, LANE), lambda h, b: (b, 0, 2 * nh + h)),
            pl.BlockSpec((1, 2 * T5_NEAR + 1, LANE, LANE), lambda h, b: (h, 0, 0, 0)),
            pl.BlockSpec((1, SEQ, LANE), lambda h, b: (b, 0, h)),
            pl.BlockSpec((1, LANE), lambda h, b: (0, 0)),
        ],
        out_specs=pl.BlockSpec((1, SEQ, LANE), lambda h, b: (b, 0, h)),
        out_shape=jax.ShapeDtypeStruct((BATCH, SEQ, DA_WIDTH), BF16),
        scratch_shapes=[
            pltpu.VMEM((2, 2 * TQ_ATTN, SEQ), F32),
            pltpu.VMEM((2, TQ_ATTN, SEQ), BF16),
            pltpu.VMEM((2, TQ_ATTN, LANE), F32),
        ],
        compiler_params=_cparams(),
        name="diff_attn",
    )(tab_log2, lam_p, qkv, qkv, qkv, tiles, rest, subln_g.reshape(1, LANE))


def _lru_kernel(x_ref, g_ref, cw_ref, cb_ref, w_ref, b_ref, lam_ref, o_ref,
                xc_ref, a3, u3, car_ref):
    S, C = SEQ, LRU_CH
    n_tiles = S // SUBLANE
    tiles_per_chunk = LRU_TC // SUBLANE

    x = x_ref[0]
    row = lax.broadcasted_iota(jnp.int32, (S, C), 0)
    xm2 = jnp.where(row >= 2, pltpu.roll(x, 2, 0), 0.0)
    xm1 = jnp.where(row >= 1, pltpu.roll(x, 1, 0), 0.0)
    xp1 = jnp.where(row < S - 1, pltpu.roll(x, S - 1, 0), 0.0)
    xc_ref[...] = (cw_ref[0, 0:1, :] * xm2 + cw_ref[0, 1:2, :] * xm1 + cw_ref[0, 2:3, :] * x
                   + cw_ref[0, 3:4, :] * xp1 + cb_ref[0])

    row8 = lax.broadcasted_iota(jnp.int32, (tiles_per_chunk, SUBLANE, C), 1)
    lam = lam_ref[0]
    half_c = (-0.5 * LRU_C) * (jnp.maximum(-lam, 0.0) + jnp.log1p(jnp.exp(-jnp.abs(lam))))

    def chunk_body(ci, carry):
        t0 = pl.multiple_of(ci * LRU_TC, LRU_TC)
        j0 = pl.multiple_of(ci * tiles_per_chunk, tiles_per_chunk)
        xc = xc_ref[pl.ds(t0, LRU_TC), :]
        xch = 0.5 * xc
        zh = jnp.dot(xc.astype(BF16), w_ref[0, 0], preferred_element_type=F32) + b_ref[0, 0]
        for direction in range(2):
            hc = half_c[direction:direction + 1, :]
            log_a = hc * jnp.tanh(zh[:, 2 * direction * C:(2 * direction + 1) * C]) + hc
            a = jnp.exp(log_a)
            mult = jnp.sqrt(-jnp.tanh(log_a) * (a * a + 1.0))
            u = mult * (jnp.tanh(zh[:, (2 * direction + 1) * C:(2 * direction + 2) * C]) + 1.0) * xch
            a = a.reshape(tiles_per_chunk, SUBLANE, C)
            u = u.reshape(tiles_per_chunk, SUBLANE, C)
            for d in (1, 2, 4):
                if direction == 1:
                    ar = pltpu.roll(a, SUBLANE - d, 1)
                    ur = pltpu.roll(u, SUBLANE - d, 1)
                    m = row8 < SUBLANE - d
                else:
                    ar = pltpu.roll(a, d, 1)
                    ur = pltpu.roll(u, d, 1)
                    m = row8 >= d
                u = jnp.where(m, a * ur + u, u)
                a = jnp.where(m, a * ar, a)
            a3[direction, pl.ds(j0, tiles_per_chunk)] = a
            u3[direction, pl.ds(j0, tiles_per_chunk)] = u
        return carry

    lax.fori_loop(0, S // LRU_TC, chunk_body, 0)

    def carry_body(jf, c):
        cf, cb = c
        jb = n_tiles - 1 - jf
        car_ref[0, jf] = cf
        car_ref[1, jb] = cb
        last = SUBLANE - 1
        return (u3[0, jf, last:, :] + a3[0, jf, last:, :] * cf,
                u3[1, jb, 0:1, :] + a3[1, jb, 0:1, :] * cb)

    zero = jnp.zeros((1, C), F32)
    lax.fori_loop(0, n_tiles, carry_body, (zero, zero), unroll=8)

    hsum = (u3[0] + a3[0] * car_ref[0]) + (u3[1] + a3[1] * car_ref[1])
    o_ref[0] = (hsum.reshape(S, C) * _silu(g_ref[0])).astype(o_ref.dtype)


def _lru_weights(w_a, b_a, w_x, b_x):
    n_e = w_a.shape[0]
    n_grp = LRU_WIDTH // LRU_CH
    bpg = LRU_CH // LRU_BLOCK
    gates = jnp.stack([w_a[:, 0], w_x[:, 0], w_a[:, 1], w_x[:, 1]], axis=1)
    gates = gates.reshape(n_e, 4, n_grp, bpg, LRU_BLOCK, LRU_BLOCK)
    w = jnp.einsum('eqgncd,nm->egncqmd', gates, jnp.eye(bpg, dtype=gates.dtype))
    w = w.reshape(n_e, n_grp, LRU_CH, 4 * LRU_CH)
    b = jnp.stack([b_a[:, 0], b_x[:, 0], b_a[:, 1], b_x[:, 1]], axis=1)
    b = b.reshape(n_e, 4, n_grp, LRU_CH).transpose(0, 2, 1, 3).reshape(n_e, n_grp, 1, 4 * LRU_CH)
    return (0.5 * w).astype(BF16), 0.5 * b


def _lru(rest, e, conv_w, conv_b, w, b, lru_lambda):
    n_grp = LRU_WIDTH // LRU_CH
    xb0 = LRU_WIDTH // LRU_CH
    gb0 = 2 * LRU_WIDTH // LRU_CH
    return pl.pallas_call(
        _lru_kernel,
        grid=(BATCH, n_grp),
        in_specs=[
            pl.BlockSpec((1, SEQ, LRU_CH), lambda bb, c: (bb, 0, xb0 + c)),
            pl.BlockSpec((1, SEQ, LRU_CH), lambda bb, c: (bb, 0, gb0 + c)),
            pl.BlockSpec((1, 4, LRU_CH), lambda bb, c: (e, 0, c)),
            pl.BlockSpec((1, 1, LRU_CH), lambda bb, c: (e, 0, c)),
            pl.BlockSpec((1, 1, LRU_CH, 4 * LRU_CH), lambda bb, c: (e, c, 0, 0)),
            pl.BlockSpec((1, 1, 1, 4 * LRU_CH), lambda bb, c: (e, c, 0, 0)),
            pl.BlockSpec((1, 2, LRU_CH), lambda bb, c: (e, 0, c)),
        ],
        out_specs=pl.BlockSpec((1, SEQ, LRU_CH), lambda bb, c: (bb, 0, c)),
        out_shape=jax.ShapeDtypeStruct((BATCH, SEQ, LRU_WIDTH), BF16),
        scratch_shapes=[
            pltpu.VMEM((SEQ, LRU_CH), F32),
            pltpu.VMEM((2, SEQ // SUBLANE, SUBLANE, LRU_CH), F32),
            pltpu.VMEM((2, SEQ // SUBLANE, SUBLANE, LRU_CH), F32),
            pltpu.VMEM((2, SEQ // SUBLANE, 1, LRU_CH), F32),
        ],
        compiler_params=_cparams(),
        name="rglru",
    )(rest, rest, conv_w, conv_b.reshape(-1, 1, LRU_WIDTH), w, b, lru_lambda)


def _na_rpb_kernel(rpb_ref, dc_ref, o_ref):
    dc = dc_ref[...]
    n_dc = 2 * NA_WIN_C - 1
    val = jnp.broadcast_to(rpb_ref[:, 0:1], o_ref.shape)
    for j in range(1, n_dc):
        val = jnp.where(dc == j, rpb_ref[:, j:j + 1], val)
    o_ref[...] = val


def _na_bias(rpb):
    n_l = rpb.shape[0]
    n_dr = 2 * NA_WIN_R - 1
    n_dc = 2 * NA_WIN_C - 1
    cols = np.arange(GRID_W)
    dc_idx = (np.clip(cols[None, :] - cols[:, None], -(NA_WIN_C - 1), NA_WIN_C - 1)
              + NA_WIN_C - 1).astype(np.int32)
    toe = pl.pallas_call(
        _na_rpb_kernel,
        out_shape=jax.ShapeDtypeStruct((n_l * NA_HEADS * n_dr, GRID_W * GRID_W), F32),
        compiler_params=_cparams(),
        name="na_rpb",
    )(rpb.reshape(n_l * NA_HEADS * n_dr, n_dc), jnp.asarray(dc_idx.reshape(1, -1)))
    col_start = np.clip(cols - NA_WIN_C // 2, 0, GRID_W - NA_WIN_C)
    col_mask = (cols[None, :] >= col_start[:, None]) & (cols[None, :] < col_start[:, None] + NA_WIN_C)
    neg = jnp.asarray(np.where(col_mask, 0.0, -np.inf).astype(np.float32))
    toe = LOG2E * toe.reshape(n_l, NA_HEADS, n_dr, GRID_W, GRID_W) + neg
    tiles = jnp.concatenate([toe[:, :, :n_dr - 1], toe[:, :, 1:]], axis=-1)
    tiles = tiles.reshape(n_l, NA_HEADS // 2, 2, n_dr - 1, GRID_W, LANE)
    return tiles.transpose(0, 1, 3, 2, 4, 5).reshape(n_l, NA_HEADS // 2, n_dr - 1, 2 * GRID_W, LANE)


def _na_kernel(q_ref, k_ref, v_ref, g_ref, bias_ref, o_ref):
    win = NA_WIN_R * GRID_W
    lane = lax.broadcasted_iota(jnp.int32, (GRID_W, LANE), 1)
    lo = lane < NA_HEAD_DIM

    def group_body(gi, carry):
        q0s, vws, ss = [], [], []
        for i in range(NA_ROWS_PER_STEP):
            r = gi * NA_ROWS_PER_STEP + i
            rs = jnp.clip(r - NA_WIN_R // 2, 0, GRID_ROWS - NA_WIN_R)
            k0 = pl.multiple_of(rs * GRID_W, GRID_W)
            q0 = pl.multiple_of(r * GRID_W, GRID_W)
            q = q_ref[0, pl.ds(q0, GRID_W), :]
            zero = jnp.zeros_like(q)
            q2 = jnp.concatenate([jnp.where(lo, q, zero), jnp.where(lo, zero, q)], axis=0)
            kw = k_ref[0, pl.ds(k0, win), :]
            dr0 = NA_WIN_R - 1 - (r - rs)
            bias = jnp.concatenate([bias_ref[0, 0, dr0 + 2 * j] for j in range(NA_WIN_R // 2)], axis=1)
            ss.append(lax.dot_general(q2, kw, NT_DIMS, preferred_element_type=F32) + bias)
            vws.append(v_ref[0, pl.ds(k0, win), :])
            q0s.append(q0)
        es, rinv = [], []
        for s in ss:
            e = jnp.exp2(s - jnp.max(s, axis=-1, keepdims=True))
            rinv.append(1.0 / jnp.sum(e, axis=-1, keepdims=True))
            es.append(e.astype(BF16))
        pvs = [jnp.dot(e, vw, preferred_element_type=F32) for e, vw in zip(es, vws)]
        for pv, ri, q0 in zip(pvs, rinv, q0s):
            o2 = pv * ri
            o = jnp.where(lo, o2[:GRID_W], o2[GRID_W:])
            o_ref[0, pl.ds(q0, GRID_W), :] = (o * _silu(g_ref[0, pl.ds(q0, GRID_W), :])).astype(o_ref.dtype)
        return carry

    lax.fori_loop(0, GRID_ROWS // NA_ROWS_PER_STEP, group_body, 0)


def _na(qkv, gate, bias, o):
    n_pair = NA_HEADS // 2
    return pl.pallas_call(
        _na_kernel,
        grid=(BATCH, n_pair),
        in_specs=[
            pl.BlockSpec((1, SEQ, LANE), lambda b, p: (b, 0, p)),
            pl.BlockSpec((1, SEQ, LANE), lambda b, p: (b, 0, n_pair + p)),
            pl.BlockSpec((1, SEQ, LANE), lambda b, p: (b, 0, 2 * n_pair + p)),
            pl.BlockSpec((1, SEQ, LANE), lambda b, p: (b, 0, p)),
            pl.BlockSpec((1, 1, 2 * NA_WIN_R - 2, 2 * GRID_W, LANE), lambda b, p: (o, p, 0, 0, 0)),
        ],
        out_specs=pl.BlockSpec((1, SEQ, LANE), lambda b, p: (b, 0, p)),
        out_shape=jax.ShapeDtypeStruct((BATCH, SEQ, NA_WIDTH), BF16),
        compiler_params=_cparams(),
        name="nbr_attn",
    )(qkv, qkv, qkv, gate, bias)


def kernel(x, c, ada_w, ada_b, norm_g, final_g, t5_table, even_w_in, even_w_out, da_lam, da_subln_g,
           lru_conv_w, lru_conv_b, lru_w_a, lru_b_a, lru_w_x, lru_b_x, lru_lambda,
           odd_w_in, odd_w_out, na_rpb):
    mod = _ada_mod(c, ada_w, ada_b)
    tab_log2 = LOG2E * t5_table
    tiles = _t5_tiles(tab_log2)
    na_bias = _na_bias(na_rpb)
    lru_w, lru_b = _lru_weights(lru_w_a, lru_b_a, lru_w_x, lru_b_x)

    def w_in_bf16(w, n_q, head_dim):
        return jnp.concatenate([w[..., :n_q] * (head_dim ** -0.5 * LOG2E), w[..., n_q:]], axis=-1).astype(BF16)

    w_in = [w_in_bf16(even_w_in, DA_WIDTH, DA_HEAD_DIM), w_in_bf16(odd_w_in, NA_WIDTH, NA_HEAD_DIM)]
    w_out = [even_w_out.astype(BF16), odd_w_out.astype(BF16)]
    widths_in = [(3 * DA_WIDTH, DA_WIDTH + 2 * LRU_WIDTH), (3 * NA_WIDTH, NA_WIDTH)]

    def in_args(l):
        n1, n2 = widths_in[l % 2]
        return dict(w_in=w_in[l % 2], norm_g=norm_g, layer_in=l, idx_in=l // 2, n1=n1, n2=n2)

    act, rest = _proj(x, mod, **in_args(0))
    for l in range(DEPTH):
        i = l // 2
        if l % 2 == 0:
            lam_init = 0.8 - 0.6 * math.exp(-0.3 * l)
            mixed = [_dattn(act, rest, tiles, tab_log2, da_lam[i], da_subln_g[i], lam_init),
                     _lru(rest, i, lru_conv_w, lru_conv_b, lru_w, lru_b, lru_lambda)]
        else:
            mixed = [_na(act, rest, na_bias, i)]
        out_args = dict(mixed=mixed, w_out=w_out[l % 2], layer_out=l, idx_out=i)
        if l + 1 < DEPTH:
            x, act, rest = _proj(x, mod, **out_args, **in_args(l + 1))
        else:
            (x,) = _proj(x, mod, **out_args, final_g=final_g)
    return x
```

```python
import functools
import math

import numpy as np
import jax
import jax.numpy as jnp
from jax import lax
from jax.experimental import pallas as pl
from jax.experimental.pallas import tpu as pltpu

F32 = jnp.float32
BF16 = jnp.bfloat16

D_MODEL = 1024
BATCH = 16
SEQ = 2048
DEPTH = 4
GRID_W = 64
EPS = 1e-6

DA_HEADS = 4
DA_HEAD_DIM = 64
DA_WIDTH = DA_HEADS * 2 * DA_HEAD_DIM
T5_BUCKETS = 32
T5_MAX_DIST = 128
LRU_WIDTH = D_MODEL // 2
LRU_BLOCKS = 8
LRU_BLOCK = LRU_WIDTH // LRU_BLOCKS
LRU_C = 8.0
NA_HEADS = 16
NA_HEAD_DIM = 64
NA_WIDTH = NA_HEADS * NA_HEAD_DIM
NA_WIN_R = 8
NA_WIN_C = 16
GRID_ROWS = SEQ // GRID_W

LANE = 128
SUBLANE = 8
VMEM_LIMIT = 48 * 1024 * 1024

TM_PROJ = 512
TQ_ATTN = 256
LRU_CH = 256
LRU_TC = 256
N_TILE_DIAG = 2 * (SEQ // LANE) - 1
T5_NEAR = 1
DA_STRIP = SUBLANE
LOG2E = math.log2(math.e)
NA_ROWS_PER_STEP = 16

NT_DIMS = (((1,), (1,)), ((), ()))


def _cparams():
    return pltpu.CompilerParams(vmem_limit_bytes=VMEM_LIMIT)


def _silu(x):
    return x * jax.nn.sigmoid(x)


def _ada_kernel(c_ref, w_ref, b_ref, o_ref):
    ca = _silu(c_ref[...])
    o_ref[0, 0] = jnp.dot(ca, w_ref[0], preferred_element_type=F32) + b_ref[0, 0]


def _ada_mod(c, ada_w, ada_b):
    return pl.pallas_call(
        _ada_kernel,
        grid=(DEPTH, 3),
        in_specs=[
            pl.BlockSpec((BATCH, D_MODEL), lambda l, j: (0, 0)),
            pl.BlockSpec((1, D_MODEL, D_MODEL), lambda l, j: (l, 0, j)),
            pl.BlockSpec((1, 1, 1, D_MODEL), lambda l, j: (l, j, 0, 0)),
        ],
        out_specs=pl.BlockSpec((1, 1, BATCH, D_MODEL), lambda l, j: (l, j, 0, 0)),
        out_shape=jax.ShapeDtypeStruct((DEPTH, 3, BATCH, D_MODEL), F32),
        compiler_params=_cparams(),
        name="ada_mod",
    )(c, ada_w, ada_b.reshape(DEPTH, 3, 1, D_MODEL))


def _rms(x, g):
    return x * lax.rsqrt(jnp.mean(x * x, axis=-1, keepdims=True) + EPS) * g


def _proj_kernel(*refs, widths, has_in, final, n1, n2, chunk):
    it = iter(refs)
    b = pl.program_id(0)
    m_refs = [next(it) for _ in widths]
    if widths:
        wo_ref, mod_ref = next(it), next(it)
    x_ref = next(it)
    if has_in:
        modn_ref, g_ref, wi_ref = next(it), next(it), next(it)
    if final:
        fg_ref = next(it)
    if widths:
        xo_ref = next(it)
    if has_in:
        o1_ref, o2_ref = next(it), next(it)

    x = x_ref[0]
    if widths:
        y, off = None, 0
        for m_ref, wd in zip(m_refs, widths):
            part = jnp.dot(m_ref[0], wo_ref[0, off:off + wd, :], preferred_element_type=F32)
            y = part if y is None else y + part
            off += wd
        x = x + mod_ref[0, 2, pl.ds(b, 1), :] * y
        xo_ref[0] = _rms(x, fg_ref[...]) if final else x
    if has_in:
        y = _rms(x, g_ref[0])
        h = (y * (1.0 + modn_ref[0, 1, pl.ds(b, 1), :]) + modn_ref[0, 0, pl.ds(b, 1), :]).astype(BF16)
        for j in range(0, n1, chunk):
            o1_ref[0, :, j:j + chunk] = jnp.dot(
                h, wi_ref[0, :, j:j + chunk], preferred_element_type=F32).astype(o1_ref.dtype)
        for j in range(0, n2, chunk):
            o2_ref[0, :, j:j + chunk] = jnp.dot(
                h, wi_ref[0, :, n1 + j:n1 + j + chunk], preferred_element_type=F32).astype(o2_ref.dtype)


def _proj(x, mod, *, mixed=(), w_out=None, layer_out=None, w_in=None, norm_g=None, layer_in=None,
          idx_out=0, idx_in=0, n1=0, n2=0, final_g=None):
    widths = tuple(m.shape[-1] for m in mixed)
    has_in = w_in is not None
    final = final_g is not None
    kern = functools.partial(_proj_kernel, widths=widths, has_in=has_in, final=final, n1=n1, n2=n2, chunk=512)
    tile = lambda wd: pl.BlockSpec((1, TM_PROJ, wd), lambda b, i: (b, i, 0))
    const = lambda shape, first: pl.BlockSpec(shape, lambda b, i: (first,) + (0,) * (len(shape) - 1),
                                              pipeline_mode=pl.Buffered(1))
    mod_spec = lambda layer: const((1, 3, BATCH, D_MODEL), layer)
    in_specs, args = [tile(wd) for wd in widths], list(mixed)
    if widths:
        in_specs += [const((1, sum(widths), D_MODEL), idx_out), mod_spec(layer_out)]
        args += [w_out, mod]
    in_specs.append(tile(D_MODEL))
    args.append(x)
    if has_in:
        in_specs += [mod_spec(layer_in), const((1, 1, D_MODEL), layer_in), const((1, D_MODEL, n1 + n2), idx_in)]
        args += [mod, norm_g.reshape(DEPTH, 1, D_MODEL), w_in]
    if final:
        in_specs.append(const((1, D_MODEL), 0))
        args.append(final_g.reshape(1, D_MODEL))
    out_specs, out_shape = [], []
    if widths:
        out_specs.append(tile(D_MODEL))
        out_shape.append(jax.ShapeDtypeStruct((BATCH, SEQ, D_MODEL), F32))
    if has_in:
        out_specs += [tile(n1), tile(n2)]
        out_shape += [jax.ShapeDtypeStruct((BATCH, SEQ, n1), BF16), jax.ShapeDtypeStruct((BATCH, SEQ, n2), F32)]
    return pl.pallas_call(
        kern,
        grid=(BATCH, SEQ // TM_PROJ),
        in_specs=in_specs,
        out_specs=out_specs,
        out_shape=out_shape,
        compiler_params=_cparams(),
        name="proj_" + ("o" if widths else "") + ("i" if has_in else "") + ("f" if final else ""),
    )(*args)


def _t5_bucket_np(rel):
    nb = T5_BUCKETS // 2
    max_exact = nb // 2
    ret = np.where(rel > 0, nb, 0)
    n = np.abs(rel)
    nf = np.maximum(n, 1).astype(np.float32)
    large = max_exact + (np.log(nf / np.float32(max_exact)) / np.float32(math.log(T5_MAX_DIST / max_exact))
                         * np.float32(nb - max_exact)).astype(np.int32)
    large = np.minimum(large, nb - 1)
    return (ret + np.where(n < max_exact, n, large)).astype(np.int32)


def _t5_tile_buckets():
    d = np.arange(N_TILE_DIAG)[:, None, None] - (SEQ // LANE - 1)
    i = np.arange(LANE)[None, :, None]
    j = np.arange(LANE)[None, None, :]
    idx = _t5_bucket_np(d * LANE + j - i)
    mid = SEQ // LANE - 1
    assert (idx[:mid - T5_NEAR] == T5_BUCKETS // 2 - 1).all() and (idx[mid + T5_NEAR + 1:] == T5_BUCKETS - 1).all()
    return idx[mid - T5_NEAR:mid + T5_NEAR + 1]


def _t5_tiles_kernel(tab_ref, idx_ref, o_ref):
    h = pl.program_id(0)
    for d in range(2 * T5_NEAR + 1):
        idx = idx_ref[d]
        val = jnp.full(idx.shape, tab_ref[h, 0], F32)
        for j in range(1, T5_BUCKETS):
            val = jnp.where(idx == j, tab_ref[h, j], val)
        o_ref[0, d] = val


def _t5_tiles(t5_table):
    idx = jnp.asarray(_t5_tile_buckets())
    n_near = 2 * T5_NEAR + 1
    return pl.pallas_call(
        _t5_tiles_kernel,
        grid=(DA_HEADS,),
        in_specs=[
            pl.BlockSpec(memory_space=pltpu.SMEM),
            pl.BlockSpec((n_near, LANE, LANE), lambda h: (0, 0, 0)),
        ],
        out_specs=pl.BlockSpec((1, n_near, LANE, LANE), lambda h: (h, 0, 0, 0)),
        out_shape=jax.ShapeDtypeStruct((DA_HEADS, n_near, LANE, LANE), F32),
        compiler_params=_cparams(),
        name="t5_tiles",
    )(t5_table, idx)


def _dattn_kernel(tab_ref, lam_ref, q_ref, k_ref, v_ref, t_ref, ga_ref, sg_ref, o_ref,
                  s_ref, a_ref, r_ref, *, lam_init):
    h = pl.program_id(0)
    n_kb = SEQ // LANE
    n_tiles = SEQ // TQ_ATTN
    c_left = tab_ref[h, T5_BUCKETS // 2 - 1]
    c_right = tab_ref[h, T5_BUCKETS - 1]

    lp = lam_ref[...]
    lam = (jnp.exp(jnp.sum(lp[0:1] * lp[1:2], axis=-1, keepdims=True))
           - jnp.exp(jnp.sum(lp[2:3] * lp[3:4], axis=-1, keepdims=True)) + lam_init)
    lane = lax.broadcasted_iota(jnp.int32, (TQ_ATTN, LANE), 1)

    def phase_a(i, slot):
        q = q_ref[0, i * TQ_ATTN:(i + 1) * TQ_ATTN, :]
        zero = jnp.zeros_like(q)
        qst = jnp.concatenate([jnp.where(lane < DA_HEAD_DIM, q, zero),
                               jnp.where(lane >= DA_HEAD_DIM, q, zero)], axis=0)
        s_ref[slot] = lax.dot_general(qst, k_ref[0], NT_DIMS, preferred_element_type=F32)

    def strip_softmax(slot, row0, qb, rr):
        cols, m_left, m_right, m_near = [], None, None, None
        for kb in range(n_kb):
            x = s_ref[slot, row0:row0 + DA_STRIP, kb * LANE:(kb + 1) * LANE]
            d = kb - qb
            if abs(d) <= T5_NEAR:
                x = x + t_ref[0, d + T5_NEAR, rr:rr + DA_STRIP, :]
                m_near = x if m_near is None else jnp.maximum(m_near, x)
            elif d < 0:
                m_left = x if m_left is None else jnp.maximum(m_left, x)
            else:
                m_right = x if m_right is None else jnp.maximum(m_right, x)
            cols.append(x)
        m = m_near
        if m_left is not None:
            m = jnp.maximum(m, m_left + c_left)
        if m_right is not None:
            m = jnp.maximum(m, m_right + c_right)
        m = jnp.max(m, axis=-1, keepdims=True)
        off_left, off_right = m - c_left, m - c_right
        es, tot = [], None
        for kb in range(n_kb):
            d = kb - qb
            off = m if abs(d) <= T5_NEAR else (off_left if d < 0 else off_right)
            e = jnp.exp2(cols[kb] - off)
            tot = e if tot is None else tot + e
            es.append(e)
        return es, jnp.sum(tot, axis=-1, keepdims=True)

    def strip_weights(i, slot, r0):
        qb = (i * TQ_ATTN + r0) // LANE
        rr = (i * TQ_ATTN + r0) % LANE
        e1, l1 = strip_softmax(slot, r0, qb, rr)
        e2, l2 = strip_softmax(slot, TQ_ATTN + r0, qb, rr)
        cc = lam * l1 * (1.0 / l2)
        r_ref[slot, r0:r0 + DA_STRIP, :] = jnp.broadcast_to(1.0 / l1, (DA_STRIP, LANE))
        return [x1 - cc * x2 for x1, x2 in zip(e1, e2)]

    def phase_b(i, slot):
        for r0 in range(0, TQ_ATTN, 2 * DA_STRIP):
            lo = strip_weights(i, slot, r0)
            hi = strip_weights(i, slot, r0 + DA_STRIP)
            for kb in range(n_kb):
                a_ref[slot, r0:r0 + 2 * DA_STRIP, kb * LANE:(kb + 1) * LANE] = (
                    jnp.concatenate([lo[kb], hi[kb]], axis=0).astype(BF16))

    def phase_c(i, slot):
        o = jnp.dot(a_ref[slot], v_ref[0], preferred_element_type=F32) * r_ref[slot]
        ms = jnp.mean(o * o, axis=-1, keepdims=True)
        o = o * lax.rsqrt(ms + EPS) * sg_ref[...] * (1.0 - lam_init)
        rows = slice(i * TQ_ATTN, (i + 1) * TQ_ATTN)
        o_ref[0, rows, :] = (o * _silu(ga_ref[0, rows, :])).astype(o_ref.dtype)

    phase_a(0, 0)
    for i in range(n_tiles):
        if i + 1 < n_tiles:
            phase_a(i + 1, (i + 1) % 2)
        phase_b(i, i % 2)
        phase_c(i, i % 2)


def _dattn(qkv, rest, tiles, tab_log2, lam_p, subln_g, lam_init):
    kern = functools.partial(_dattn_kernel, lam_init=lam_init)
    nh = DA_HEADS
    return pl.pallas_call(
        kern,
        grid=(nh, BATCH),
        in_specs=[
            pl.BlockSpec(memory_space=pltpu.SMEM),
            pl.BlockSpec((4, DA_HEAD_DIM), lambda h, b: (0, 0)),
            pl.BlockSpec((1, SEQ, LANE), lambda h, b: (b, 0, h)),
            pl.BlockSpec((1, SEQ, LANE), lambda h, b: (b, 0, nh + h)),
            pl.BlockSpec((1, SEQ, LANE), lambda h, b: (b, 0, 2 * nh + h)),
            pl.BlockSpec((1, 2 * T5_NEAR + 1, LANE, LANE), lambda h, b: (h, 0, 0, 0)),
            pl.BlockSpec((1, SEQ, LANE), lambda h, b: (b, 0, h)),
            pl.BlockSpec((1, LANE), lambda h, b: (0, 0)),
        ],
        out_specs=pl.BlockSpec((1, SEQ, LANE), lambda h, b: (b, 0, h)),
        out_shape=jax.ShapeDtypeStruct((BATCH, SEQ, DA_WIDTH), BF16),
        scratch_shapes=[
            pltpu.VMEM((2, 2 * TQ_ATTN, SEQ), F32),
            pltpu.VMEM((2, TQ_ATTN, SEQ), BF16),
            pltpu.VMEM((2, TQ_ATTN, LANE), F32),
        ],
        compiler_params=_cparams(),
        name="diff_attn",
    )(tab_log2, lam_p, qkv, qkv, qkv, tiles, rest, subln_g.reshape(1, LANE))


def _lru_kernel(x_ref, g_ref, cw_ref, cb_ref, w_ref, b_ref, lam_ref, o_ref,
                xc_ref, a3, u3, car_ref):
    S, C = SEQ, LRU_CH
    n_tiles = S // SUBLANE
    tiles_per_chunk = LRU_TC // SUBLANE

    x = x_ref[0]
    row = lax.broadcasted_iota(jnp.int32, (S, C), 0)
    xm2 = jnp.where(row >= 2, pltpu.roll(x, 2, 0), 0.0)
    xm1 = jnp.where(row >= 1, pltpu.roll(x, 1, 0), 0.0)
    xp1 = jnp.where(row < S - 1, pltpu.roll(x, S - 1, 0), 0.0)
    xc_ref[...] = (cw_ref[0, 0:1, :] * xm2 + cw_ref[0, 1:2, :] * xm1 + cw_ref[0, 2:3, :] * x
                   + cw_ref[0, 3:4, :] * xp1 + cb_ref[0])

    row8 = lax.broadcasted_iota(jnp.int32, (tiles_per_chunk, SUBLANE, C), 1)
    lam = lam_ref[0]
    half_c = (-0.5 * LRU_C) * (jnp.maximum(-lam, 0.0) + jnp.log1p(jnp.exp(-jnp.abs(lam))))

    def chunk_body(ci, carry):
        t0 = pl.multiple_of(ci * LRU_TC, LRU_TC)
        j0 = pl.multiple_of(ci * tiles_per_chunk, tiles_per_chunk)
        xc = xc_ref[pl.ds(t0, LRU_TC), :]
        xch = 0.5 * xc
        zh = jnp.dot(xc.astype(BF16), w_ref[0, 0], preferred_element_type=F32) + b_ref[0, 0]
        for direction in range(2):
            hc = half_c[direction:direction + 1, :]
            log_a = hc * jnp.tanh(zh[:, 2 * direction * C:(2 * direction + 1) * C]) + hc
            a = jnp.exp(log_a)
            mult = jnp.sqrt(-jnp.tanh(log_a) * (a * a + 1.0))
            u = mult * (jnp.tanh(zh[:, (2 * direction + 1) * C:(2 * direction + 2) * C]) + 1.0) * xch
            a = a.reshape(tiles_per_chunk, SUBLANE, C)
            u = u.reshape(tiles_per_chunk, SUBLANE, C)
            for d in (1, 2, 4):
                if direction == 1:
                    ar = pltpu.roll(a, SUBLANE - d, 1)
                    ur = pltpu.roll(u, SUBLANE - d, 1)
                    m = row8 < SUBLANE - d
                else:
                    ar = pltpu.roll(a, d, 1)
                    ur = pltpu.roll(u, d, 1)
                    m = row8 >= d
                u = jnp.where(m, a * ur + u, u)
                a = jnp.where(m, a * ar, a)
            a3[direction, pl.ds(j0, tiles_per_chunk)] = a
            u3[direction, pl.ds(j0, tiles_per_chunk)] = u
        return carry

    lax.fori_loop(0, S // LRU_TC, chunk_body, 0)

    def carry_body(jf, c):
        cf, cb = c
        jb = n_tiles - 1 - jf
        car_ref[0, jf] = cf
        car_ref[1, jb] = cb
        last = SUBLANE - 1
        return (u3[0, jf, last:, :] + a3[0, jf, last:, :] * cf,
                u3[1, jb, 0:1, :] + a3[1, jb, 0:1, :] * cb)

    zero = jnp.zeros((1, C), F32)
    lax.fori_loop(0, n_tiles, carry_body, (zero, zero), unroll=8)

    hsum = (u3[0] + a3[0] * car_ref[0]) + (u3[1] + a3[1] * car_ref[1])
    o_ref[0] = (hsum.reshape(S, C) * _silu(g_ref[0])).astype(o_ref.dtype)


def _lru_weights(w_a, b_a, w_x, b_x):
    n_e = w_a.shape[0]
    n_grp = LRU_WIDTH // LRU_CH
    bpg = LRU_CH // LRU_BLOCK
    gates = jnp.stack([w_a[:, 0], w_x[:, 0], w_a[:, 1], w_x[:, 1]], axis=1)
    gates = gates.reshape(n_e, 4, n_grp, bpg, LRU_BLOCK, LRU_BLOCK)
    w = jnp.einsum('eqgncd,nm->egncqmd', gates, jnp.eye(bpg, dtype=gates.dtype))
    w = w.reshape(n_e, n_grp, LRU_CH, 4 * LRU_CH)
    b = jnp.stack([b_a[:, 0], b_x[:, 0], b_a[:, 1], b_x[:, 1]], axis=1)
    b = b.reshape(n_e, 4, n_grp, LRU_CH).transpose(0, 2, 1, 3).reshape(n_e, n_grp, 1, 4 * LRU_CH)
    return (0.5 * w).astype(BF16), 0.5 * b


def _lru(rest, e, conv_w, conv_b, w, b, lru_lambda):
    n_grp = LRU_WIDTH // LRU_CH
    xb0 = LRU_WIDTH // LRU_CH
    gb0 = 2 * LRU_WIDTH // LRU_CH
    return pl.pallas_call(
        _lru_kernel,
        grid=(BATCH, n_grp),
        in_specs=[
            pl.BlockSpec((1, SEQ, LRU_CH), lambda bb, c: (bb, 0, xb0 + c)),
            pl.BlockSpec((1, SEQ, LRU_CH), lambda bb, c: (bb, 0, gb0 + c)),
            pl.BlockSpec((1, 4, LRU_CH), lambda bb, c: (e, 0, c)),
            pl.BlockSpec((1, 1, LRU_CH), lambda bb, c: (e, 0, c)),
            pl.BlockSpec((1, 1, LRU_CH, 4 * LRU_CH), lambda bb, c: (e, c, 0, 0)),
            pl.BlockSpec((1, 1, 1, 4 * LRU_CH), lambda bb, c: (e, c, 0, 0)),
            pl.BlockSpec((1, 2, LRU_CH), lambda bb, c: (e, 0, c)),
        ],
        out_specs=pl.BlockSpec((1, SEQ, LRU_CH), lambda bb, c: (bb, 0, c)),
        out_shape=jax.ShapeDtypeStruct((BATCH, SEQ, LRU_WIDTH), BF16),
        scratch_shapes=[
            pltpu.VMEM((SEQ, LRU_CH), F32),
            pltpu.VMEM((2, SEQ // SUBLANE, SUBLANE, LRU_CH), F32),
            pltpu.VMEM((2, SEQ // SUBLANE, SUBLANE, LRU_CH), F32),
            pltpu.VMEM((2, SEQ // SUBLANE, 1, LRU_CH), F32),
        ],
        compiler_params=_cparams(),
        name="rglru",
    )(rest, rest, conv_w, conv_b.reshape(-1, 1, LRU_WIDTH), w, b, lru_lambda)


def _na_rpb_kernel(rpb_ref, dc_ref, o_ref):
    dc = dc_ref[...]
    n_dc = 2 * NA_WIN_C - 1
    val = jnp.broadcast_to(rpb_ref[:, 0:1], o_ref.shape)
    for j in range(1, n_dc):
        val = jnp.where(dc == j, rpb_ref[:, j:j + 1], val)
    o_ref[...] = val


def _na_bias(rpb):
    n_l = rpb.shape[0]
    n_dr = 2 * NA_WIN_R - 1
    n_dc = 2 * NA_WIN_C - 1
    cols = np.arange(GRID_W)
    dc_idx = (np.clip(cols[None, :] - cols[:, None], -(NA_WIN_C - 1), NA_WIN_C - 1)
              + NA_WIN_C - 1).astype(np.int32)
    toe = pl.pallas_call(
        _na_rpb_kernel,
        out_shape=jax.ShapeDtypeStruct((n_l * NA_HEADS * n_dr, GRID_W * GRID_W), F32),
        compiler_params=_cparams(),
        name="na_rpb",
    )(rpb.reshape(n_l * NA_HEADS * n_dr, n_dc), jnp.asarray(dc_idx.reshape(1, -1)))
    col_start = np.clip(cols - NA_WIN_C // 2, 0, GRID_W - NA_WIN_C)
    col_mask = (cols[None, :] >= col_start[:, None]) & (cols[None, :] < col_start[:, None] + NA_WIN_C)
    neg = jnp.asarray(np.where(col_mask, 0.0, -np.inf).astype(np.float32))
    toe = LOG2E * toe.reshape(n_l, NA_HEADS, n_dr, GRID_W, GRID_W) + neg
    tiles = jnp.concatenate([toe[:, :, :n_dr - 1], toe[:, :, 1:]], axis=-1)
    tiles = tiles.reshape(n_l, NA_HEADS // 2, 2, n_dr - 1, GRID_W, LANE)
    return tiles.transpose(0, 1, 3, 2, 4, 5).reshape(n_l, NA_HEADS // 2, n_dr - 1, 2 * GRID_W, LANE)


def _na_kernel(q_ref, k_ref, v_ref, g_ref, bias_ref, o_ref):
    win = NA_WIN_R * GRID_W
    lane = lax.broadcasted_iota(jnp.int32, (GRID_W, LANE), 1)
    lo = lane < NA_HEAD_DIM

    def group_body(gi, carry):
        q0s, vws, ss = [], [], []
        for i in range(NA_ROWS_PER_STEP):
            r = gi * NA_ROWS_PER_STEP + i
            rs = jnp.clip(r - NA_WIN_R // 2, 0, GRID_ROWS - NA_WIN_R)
            k0 = pl.multiple_of(rs * GRID_W, GRID_W)
            q0 = pl.multiple_of(r * GRID_W, GRID_W)
            q = q_ref[0, pl.ds(q0, GRID_W), :]
            zero = jnp.zeros_like(q)
            q2 = jnp.concatenate([jnp.where(lo, q, zero), jnp.where(lo, zero, q)], axis=0)
            kw = k_ref[0, pl.ds(k0, win), :]
            dr0 = NA_WIN_R - 1 - (r - rs)
            bias = jnp.concatenate([bias_ref[0, 0, dr0 + 2 * j] for j in range(NA_WIN_R // 2)], axis=1)
            ss.append(lax.dot_general(q2, kw, NT_DIMS, preferred_element_type=F32) + bias)
            vws.append(v_ref[0, pl.ds(k0, win), :])
            q0s.append(q0)
        es, rinv = [], []
        for s in ss:
            e = jnp.exp2(s - jnp.max(s, axis=-1, keepdims=True))
            rinv.append(1.0 / jnp.sum(e, axis=-1, keepdims=True))
            es.append(e.astype(BF16))
        pvs = [jnp.dot(e, vw, preferred_element_type=F32) for e, vw in zip(es, vws)]
        for pv, ri, q0 in zip(pvs, rinv, q0s):
            o2 = pv * ri
            o = jnp.where(lo, o2[:GRID_W], o2[GRID_W:])
            o_ref[0, pl.ds(q0, GRID_W), :] = (o * _silu(g_ref[0, pl.ds(q0, GRID_W), :])).astype(o_ref.dtype)
        return carry

    lax.fori_loop(0, GRID_ROWS // NA_ROWS_PER_STEP, group_body, 0)


def _na(qkv, gate, bias, o):
    n_pair = NA_HEADS // 2
    return pl.pallas_call(
        _na_kernel,
        grid=(BATCH, n_pair),
        in_specs=[
            pl.BlockSpec((1, SEQ, LANE), lambda b, p: (b, 0, p)),
            pl.BlockSpec((1, SEQ, LANE), lambda b, p: (b, 0, n_pair + p)),
            pl.BlockSpec((1, SEQ, LANE), lambda b, p: (b, 0, 2 * n_pair + p)),
            pl.BlockSpec((1, SEQ, LANE), lambda b, p: (b, 0, p)),
            pl.BlockSpec((1, 1, 2 * NA_WIN_R - 2, 2 * GRID_W, LANE), lambda b, p: (o, p, 0, 0, 0)),
        ],
        out_specs=pl.BlockSpec((1, SEQ, LANE), lambda b, p: (b, 0, p)),
        out_shape=jax.ShapeDtypeStruct((BATCH, SEQ, NA_WIDTH), BF16),
        compiler_params=_cparams(),
        name="nbr_attn",
    )(qkv, qkv, qkv, gate, bias)


def kernel(x, c, ada_w, ada_b, norm_g, final_g, t5_table, even_w_in, even_w_out, da_lam, da_subln_g,
           lru_conv_w, lru_conv_b, lru_w_a, lru_b_a, lru_w_x, lru_b_x, lru_lambda,
           odd_w_in, odd_w_out, na_rpb):
    mod = _ada_mod(c, ada_w, ada_b)
    tab_log2 = LOG2E * t5_table
    tiles = _t5_tiles(tab_log2)
    na_bias = _na_bias(na_rpb)
    lru_w, lru_b = _lru_weights(lru_w_a, lru_b_a, lru_w_x, lru_b_x)

    def w_in_bf16(w, n_q, head_dim):
        col_scale = np.ones((w.shape[-1],), np.float32)
        col_scale[:n_q] = head_dim ** -0.5 * LOG2E
        return (w * col_scale).astype(BF16)

    w_in = [w_in_bf16(even_w_in, DA_WIDTH, DA_HEAD_DIM), w_in_bf16(odd_w_in, NA_WIDTH, NA_HEAD_DIM)]
    w_out = [even_w_out.astype(BF16), odd_w_out.astype(BF16)]
    widths_in = [(3 * DA_WIDTH, DA_WIDTH + 2 * LRU_WIDTH), (3 * NA_WIDTH, NA_WIDTH)]

    def in_args(l):
        n1, n2 = widths_in[l % 2]
        return dict(w_in=w_in[l % 2], norm_g=norm_g, layer_in=l, idx_in=l // 2, n1=n1, n2=n2)

    act, rest = _proj(x, mod, **in_args(0))
    for l in range(DEPTH):
        i = l // 2
        if l % 2 == 0:
            lam_init = 0.8 - 0.6 * math.exp(-0.3 * l)
            mixed = [_dattn(act, rest, tiles, tab_log2, da_lam[i], da_subln_g[i], lam_init),
                     _lru(rest, i, lru_conv_w, lru_conv_b, lru_w, lru_b, lru_lambda)]
        else:
            mixed = [_na(act, rest, na_bias, i)]
        out_args = dict(mixed=mixed, w_out=w_out[l % 2], layer_out=l, idx_out=i)
        if l + 1 < DEPTH:
            x, act, rest = _proj(x, mod, **out_args, **in_args(l + 1))
        else:
            (x,) = _proj(x, mod, **out_args, final_g=final_g)
    return x
```

```python
import functools
import math

import numpy as np
import jax
import jax.numpy as jnp
from jax import lax
from jax.experimental import pallas as pl
from jax.experimental.pallas import tpu as pltpu

F32 = jnp.float32
BF16 = jnp.bfloat16

D_MODEL = 1024
BATCH = 16
SEQ = 2048
DEPTH = 4
GRID_W = 64
EPS = 1e-6

DA_HEADS = 4
DA_HEAD_DIM = 64
DA_WIDTH = DA_HEADS * 2 * DA_HEAD_DIM
T5_BUCKETS = 32
T5_MAX_DIST = 128
LRU_WIDTH = D_MODEL // 2
LRU_BLOCKS = 8
LRU_BLOCK = LRU_WIDTH // LRU_BLOCKS
LRU_C = 8.0
NA_HEADS = 16
NA_HEAD_DIM = 64
NA_WIDTH = NA_HEADS * NA_HEAD_DIM
NA_WIN_R = 8
NA_WIN_C = 16
GRID_ROWS = SEQ // GRID_W

LANE = 128
SUBLANE = 8
VMEM_LIMIT = 48 * 1024 * 1024

TM_PROJ = 512
TN_PROJ = 512
TQ_ATTN = 256
LRU_CH = 256
LRU_TC = 256
N_TILE_DIAG = 2 * (SEQ // LANE) - 1
T5_NEAR = 1
DA_STRIP = SUBLANE
LOG2E = math.log2(math.e)
NA_ROWS_PER_STEP = 16

NT_DIMS = (((1,), (1,)), ((), ()))


def _cparams():
    return pltpu.CompilerParams(vmem_limit_bytes=VMEM_LIMIT)


def _silu(x):
    return x * jax.nn.sigmoid(x)


def _ada_kernel(c_ref, w_ref, b_ref, o_ref):
    ca = _silu(c_ref[...])
    o_ref[0, 0] = jnp.dot(ca, w_ref[0], preferred_element_type=F32) + b_ref[0, 0]


def _ada_mod(c, ada_w, ada_b):
    return pl.pallas_call(
        _ada_kernel,
        grid=(DEPTH, 3),
        in_specs=[
            pl.BlockSpec((BATCH, D_MODEL), lambda l, j: (0, 0)),
            pl.BlockSpec((1, D_MODEL, D_MODEL), lambda l, j: (l, 0, j)),
            pl.BlockSpec((1, 1, 1, D_MODEL), lambda l, j: (l, j, 0, 0)),
        ],
        out_specs=pl.BlockSpec((1, 1, BATCH, D_MODEL), lambda l, j: (l, j, 0, 0)),
        out_shape=jax.ShapeDtypeStruct((DEPTH, 3, BATCH, D_MODEL), F32),
        compiler_params=_cparams(),
        name="ada_mod",
    )(c, ada_w, ada_b.reshape(DEPTH, 3, 1, D_MODEL))


def _rms(x, g):
    return x * lax.rsqrt(jnp.mean(x * x, axis=-1, keepdims=True) + EPS) * g


def _proj_kernel(*refs, widths, has_in, final, n1, n2, chunk):
    it = iter(refs)
    b = pl.program_id(0)
    m_refs = [next(it) for _ in widths]
    if widths:
        wo_ref, mod_ref = next(it), next(it)
    x_ref = next(it)
    if has_in:
        modn_ref, g_ref, wi_ref = next(it), next(it), next(it)
    if final:
        fg_ref = next(it)
    if widths:
        xo_ref = next(it)
    if has_in:
        o1_ref, o2_ref = next(it), next(it)

    x = x_ref[0]
    if widths:
        y, off = None, 0
        for m_ref, wd in zip(m_refs, widths):
            part = jnp.dot(m_ref[0], wo_ref[0, off:off + wd, :], preferred_element_type=F32)
            y = part if y is None else y + part
            off += wd
        x = x + mod_ref[0, 2, pl.ds(b, 1), :] * y
        xo_ref[0] = _rms(x, fg_ref[...]) if final else x
    if has_in:
        y = _rms(x, g_ref[0])
        h = (y * (1.0 + modn_ref[0, 1, pl.ds(b, 1), :]) + modn_ref[0, 0, pl.ds(b, 1), :]).astype(BF16)
        for j in range(0, n1, chunk):
            o1_ref[0, :, j:j + chunk] = jnp.dot(
                h, wi_ref[0, :, j:j + chunk], preferred_element_type=F32).astype(o1_ref.dtype)
        for j in range(0, n2, chunk):
            o2_ref[0, :, j:j + chunk] = jnp.dot(
                h, wi_ref[0, :, n1 + j:n1 + j + chunk], preferred_element_type=F32).astype(o2_ref.dtype)


def _proj(x, mod, *, mixed=(), w_out=None, layer_out=None, w_in=None, norm_g=None, layer_in=None,
          idx_out=0, idx_in=0, n1=0, n2=0, final_g=None):
    widths = tuple(m.shape[-1] for m in mixed)
    has_in = w_in is not None
    final = final_g is not None
    kern = functools.partial(_proj_kernel, widths=widths, has_in=has_in, final=final, n1=n1, n2=n2,
                             chunk=TN_PROJ)
    tile = lambda wd: pl.BlockSpec((1, TM_PROJ, wd), lambda b, i: (b, i, 0))
    const = lambda shape, first: pl.BlockSpec(shape, lambda b, i: (first,) + (0,) * (len(shape) - 1),
                                              pipeline_mode=pl.Buffered(1))
    mod_spec = lambda layer: const((1, 3, BATCH, D_MODEL), layer)
    in_specs, args = [tile(wd) for wd in widths], list(mixed)
    if widths:
        in_specs += [const((1, sum(widths), D_MODEL), idx_out), mod_spec(layer_out)]
        args += [w_out, mod]
    in_specs.append(tile(D_MODEL))
    args.append(x)
    if has_in:
        in_specs += [mod_spec(layer_in), const((1, 1, D_MODEL), layer_in), const((1, D_MODEL, n1 + n2), idx_in)]
        args += [mod, norm_g.reshape(DEPTH, 1, D_MODEL), w_in]
    if final:
        in_specs.append(const((1, D_MODEL), 0))
        args.append(final_g.reshape(1, D_MODEL))
    out_specs, out_shape = [], []
    if widths:
        out_specs.append(tile(D_MODEL))
        out_shape.append(jax.ShapeDtypeStruct((BATCH, SEQ, D_MODEL), F32))
    if has_in:
        out_specs += [tile(n1), tile(n2)]
        out_shape += [jax.ShapeDtypeStruct((BATCH, SEQ, n1), BF16), jax.ShapeDtypeStruct((BATCH, SEQ, n2), F32)]
    return pl.pallas_call(
        kern,
        grid=(BATCH, SEQ // TM_PROJ),
        in_specs=in_specs,
        out_specs=out_specs,
        out_shape=out_shape,
        compiler_params=_cparams(),
        name="proj_" + ("o" if widths else "") + ("i" if has_in else "") + ("f" if final else ""),
    )(*args)


def _t5_bucket_np(rel):
    nb = T5_BUCKETS // 2
    max_exact = nb // 2
    ret = np.where(rel > 0, nb, 0)
    n = np.abs(rel)
    nf = np.maximum(n, 1).astype(np.float32)
    large = max_exact + (np.log(nf / np.float32(max_exact)) / np.float32(math.log(T5_MAX_DIST / max_exact))
                         * np.float32(nb - max_exact)).astype(np.int32)
    large = np.minimum(large, nb - 1)
    return (ret + np.where(n < max_exact, n, large)).astype(np.int32)


def _t5_tile_buckets():
    d = np.arange(N_TILE_DIAG)[:, None, None] - (SEQ // LANE - 1)
    i = np.arange(LANE)[None, :, None]
    j = np.arange(LANE)[None, None, :]
    idx = _t5_bucket_np(d * LANE + j - i)
    mid = SEQ // LANE - 1
    assert (idx[:mid - T5_NEAR] == T5_BUCKETS // 2 - 1).all() and (idx[mid + T5_NEAR + 1:] == T5_BUCKETS - 1).all()
    return idx[mid - T5_NEAR:mid + T5_NEAR + 1]


def _t5_tiles_kernel(tab_ref, idx_ref, o_ref):
    h = pl.program_id(0)
    for d in range(2 * T5_NEAR + 1):
        idx = idx_ref[d]
        val = jnp.full(idx.shape, tab_ref[h, 0], F32)
        for j in range(1, T5_BUCKETS):
            val = jnp.where(idx == j, tab_ref[h, j], val)
        o_ref[0, d] = val


def _t5_tiles(t5_table):
    idx = jnp.asarray(_t5_tile_buckets())
    n_near = 2 * T5_NEAR + 1
    return pl.pallas_call(
        _t5_tiles_kernel,
        grid=(DA_HEADS,),
        in_specs=[
            pl.BlockSpec(memory_space=pltpu.SMEM),
            pl.BlockSpec((n_near, LANE, LANE), lambda h: (0, 0, 0)),
        ],
        out_specs=pl.BlockSpec((1, n_near, LANE, LANE), lambda h: (h, 0, 0, 0)),
        out_shape=jax.ShapeDtypeStruct((DA_HEADS, n_near, LANE, LANE), F32),
        compiler_params=_cparams(),
        name="t5_tiles",
    )(t5_table, idx)


def _dattn_kernel(tab_ref, lam_ref, q_ref, k_ref, v_ref, t_ref, ga_ref, sg_ref, o_ref,
                  s_ref, a_ref, r_ref, *, lam_init):
    h = pl.program_id(0)
    n_kb = SEQ // LANE
    n_tiles = SEQ // TQ_ATTN
    c_left = tab_ref[h, T5_BUCKETS // 2 - 1]
    c_right = tab_ref[h, T5_BUCKETS - 1]

    lp = lam_ref[...]
    lam = (jnp.exp(jnp.sum(lp[0:1] * lp[1:2], axis=-1, keepdims=True))
           - jnp.exp(jnp.sum(lp[2:3] * lp[3:4], axis=-1, keepdims=True)) + lam_init)
    lane = lax.broadcasted_iota(jnp.int32, (TQ_ATTN, LANE), 1)

    def phase_a(i, slot):
        q = q_ref[0, i * TQ_ATTN:(i + 1) * TQ_ATTN, :]
        zero = jnp.zeros_like(q)
        qst = jnp.concatenate([jnp.where(lane < DA_HEAD_DIM, q, zero),
                               jnp.where(lane >= DA_HEAD_DIM, q, zero)], axis=0)
        s_ref[slot] = lax.dot_general(qst, k_ref[0], NT_DIMS, preferred_element_type=F32)

    def strip_softmax(slot, row0, qb, rr):
        cols, m_left, m_right, m_near = [], None, None, None
        for kb in range(n_kb):
            x = s_ref[slot, row0:row0 + DA_STRIP, kb * LANE:(kb + 1) * LANE]
            d = kb - qb
            if abs(d) <= T5_NEAR:
                x = x + t_ref[0, d + T5_NEAR, rr:rr + DA_STRIP, :]
                m_near = x if m_near is None else jnp.maximum(m_near, x)
            elif d < 0:
                m_left = x if m_left is None else jnp.maximum(m_left, x)
            else:
                m_right = x if m_right is None else jnp.maximum(m_right, x)
            cols.append(x)
        m = m_near
        if m_left is not None:
            m = jnp.maximum(m, m_left + c_left)
        if m_right is not None:
            m = jnp.maximum(m, m_right + c_right)
        m = jnp.max(m, axis=-1, keepdims=True)
        off_left, off_right = m - c_left, m - c_right
        es, tot = [], None
        for kb in range(n_kb):
            d = kb - qb
            off = m if abs(d) <= T5_NEAR else (off_left if d < 0 else off_right)
            e = jnp.exp2(cols[kb] - off)
            tot = e if tot is None else tot + e
            es.append(e)
        return es, jnp.sum(tot, axis=-1, keepdims=True)

    def strip_weights(i, slot, r0):
        qb = (i * TQ_ATTN + r0) // LANE
        rr = (i * TQ_ATTN + r0) % LANE
        e1, l1 = strip_softmax(slot, r0, qb, rr)
        e2, l2 = strip_softmax(slot, TQ_ATTN + r0, qb, rr)
        cc = lam * l1 * (1.0 / l2)
        r_ref[slot, r0:r0 + DA_STRIP, :] = jnp.broadcast_to(1.0 / l1, (DA_STRIP, LANE))
        return [x1 - cc * x2 for x1, x2 in zip(e1, e2)]

    def phase_b(i, slot):
        for r0 in range(0, TQ_ATTN, 2 * DA_STRIP):
            lo = strip_weights(i, slot, r0)
            hi = strip_weights(i, slot, r0 + DA_STRIP)
            for kb in range(n_kb):
                a_ref[slot, r0:r0 + 2 * DA_STRIP, kb * LANE:(kb + 1) * LANE] = (
                    jnp.concatenate([lo[kb], hi[kb]], axis=0).astype(BF16))

    def phase_c(i, slot):
        o = jnp.dot(a_ref[slot], v_ref[0], preferred_element_type=F32) * r_ref[slot]
        ms = jnp.mean(o * o, axis=-1, keepdims=True)
        o = o * lax.rsqrt(ms + EPS) * sg_ref[...] * (1.0 - lam_init)
        rows = slice(i * TQ_ATTN, (i + 1) * TQ_ATTN)
        o_ref[0, rows, :] = (o * _silu(ga_ref[0, rows, :])).astype(o_ref.dtype)

    phase_a(0, 0)
    for i in range(n_tiles):
        if i + 1 < n_tiles:
            phase_a(i + 1, (i + 1) % 2)
        phase_b(i, i % 2)
        phase_c(i, i % 2)


def _dattn(qkv, rest, tiles, tab_log2, lam_p, subln_g, lam_init):
    kern = functools.partial(_dattn_kernel, lam_init=lam_init)
    nh = DA_HEADS
    return pl.pallas_call(
        kern,
        grid=(nh, BATCH),
        in_specs=[
            pl.BlockSpec(memory_space=pltpu.SMEM),
            pl.BlockSpec((4, DA_HEAD_DIM), lambda h, b: (0, 0)),
            pl.BlockSpec((1, SEQ, LANE), lambda h, b: (b, 0, h)),
            pl.BlockSpec((1, SEQ, LANE), lambda h, b: (b, 0, nh + h)),
            pl.BlockSpec((1, SEQ, LANE), lambda h, b: (b, 0, 2 * nh + h)),
            pl.BlockSpec((1, 2 * T5_NEAR + 1, LANE, LANE), lambda h, b: (h, 0, 0, 0)),
            pl.BlockSpec((1, SEQ, LANE), lambda h, b: (b, 0, h)),
            pl.BlockSpec((1, LANE), lambda h, b: (0, 0)),
        ],
        out_specs=pl.BlockSpec((1, SEQ, LANE), lambda h, b: (b, 0, h)),
        out_shape=jax.ShapeDtypeStruct((BATCH, SEQ, DA_WIDTH), BF16),
        scratch_shapes=[
            pltpu.VMEM((2, 2 * TQ_ATTN, SEQ), F32),
            pltpu.VMEM((2, TQ_ATTN, SEQ), BF16),
            pltpu.VMEM((2, TQ_ATTN, LANE), F32),
        ],
        compiler_params=_cparams(),
        name="diff_attn",
    )(tab_log2, lam_p, qkv, qkv, qkv, tiles, rest, subln_g.reshape(1, LANE))


def _lru_kernel(x_ref, g_ref, cw_ref, cb_ref, w_ref, b_ref, lam_ref, o_ref,
                xc_ref, a3, u3, car_ref):
    S, C = SEQ, LRU_CH
    n_tiles = S // SUBLANE
    tiles_per_chunk = LRU_TC // SUBLANE

    x = x_ref[0]
    row = lax.broadcasted_iota(jnp.int32, (S, C), 0)
    xm2 = jnp.where(row >= 2, pltpu.roll(x, 2, 0), 0.0)
    xm1 = jnp.where(row >= 1, pltpu.roll(x, 1, 0), 0.0)
    xp1 = jnp.where(row < S - 1, pltpu.roll(x, S - 1, 0), 0.0)
    xc_ref[...] = (cw_ref[0, 0:1, :] * xm2 + cw_ref[0, 1:2, :] * xm1 + cw_ref[0, 2:3, :] * x
                   + cw_ref[0, 3:4, :] * xp1 + cb_ref[0])

    row8 = lax.broadcasted_iota(jnp.int32, (tiles_per_chunk, SUBLANE, C), 1)
    lam = lam_ref[0]
    half_c = (-0.5 * LRU_C) * (jnp.maximum(-lam, 0.0) + jnp.log1p(jnp.exp(-jnp.abs(lam))))

    def chunk_body(ci, carry):
        t0 = pl.multiple_of(ci * LRU_TC, LRU_TC)
        j0 = pl.multiple_of(ci * tiles_per_chunk, tiles_per_chunk)
        xc = xc_ref[pl.ds(t0, LRU_TC), :]
        xch = 0.5 * xc
        zh = jnp.dot(xc.astype(BF16), w_ref[0, 0], preferred_element_type=F32) + b_ref[0, 0]
        for direction in range(2):
            hc = half_c[direction:direction + 1, :]
            log_a = hc * jnp.tanh(zh[:, 2 * direction * C:(2 * direction + 1) * C]) + hc
            a = jnp.exp(log_a)
            m2 = -jnp.tanh(log_a) * (a * a + 1.0)
            mult = jnp.where(m2 > 0.0, m2 * lax.rsqrt(m2), 0.0)
            u = mult * (jnp.tanh(zh[:, (2 * direction + 1) * C:(2 * direction + 2) * C]) + 1.0) * xch
            a = a.reshape(tiles_per_chunk, SUBLANE, C)
            u = u.reshape(tiles_per_chunk, SUBLANE, C)
            for d in (1, 2, 4):
                if direction == 1:
                    ar = pltpu.roll(a, SUBLANE - d, 1)
                    ur = pltpu.roll(u, SUBLANE - d, 1)
                    m = row8 < SUBLANE - d
                else:
                    ar = pltpu.roll(a, d, 1)
                    ur = pltpu.roll(u, d, 1)
                    m = row8 >= d
                u = jnp.where(m, a * ur + u, u)
                a = jnp.where(m, a * ar, a)
            a3[direction, pl.ds(j0, tiles_per_chunk)] = a
            u3[direction, pl.ds(j0, tiles_per_chunk)] = u
        return carry

    lax.fori_loop(0, S // LRU_TC, chunk_body, 0)

    def carry_body(jf, c):
        cf, cb = c
        jb = n_tiles - 1 - jf
        car_ref[0, jf] = cf
        car_ref[1, jb] = cb
        last = SUBLANE - 1
        return (u3[0, jf, last:, :] + a3[0, jf, last:, :] * cf,
                u3[1, jb, 0:1, :] + a3[1, jb, 0:1, :] * cb)

    zero = jnp.zeros((1, C), F32)
    lax.fori_loop(0, n_tiles, carry_body, (zero, zero), unroll=8)

    hsum = (u3[0] + a3[0] * car_ref[0]) + (u3[1] + a3[1] * car_ref[1])
    o_ref[0] = (hsum.reshape(S, C) * _silu(g_ref[0])).astype(o_ref.dtype)


def _lru_weights(w_a, b_a, w_x, b_x):
    n_e = w_a.shape[0]
    n_grp = LRU_WIDTH // LRU_CH
    bpg = LRU_CH // LRU_BLOCK
    gates = jnp.stack([w_a[:, 0], w_x[:, 0], w_a[:, 1], w_x[:, 1]], axis=1)
    gates = gates.reshape(n_e, 4, n_grp, bpg, LRU_BLOCK, LRU_BLOCK)
    w = jnp.einsum('eqgncd,nm->egncqmd', gates, jnp.eye(bpg, dtype=gates.dtype))
    w = w.reshape(n_e, n_grp, LRU_CH, 4 * LRU_CH)
    b = jnp.stack([b_a[:, 0], b_x[:, 0], b_a[:, 1], b_x[:, 1]], axis=1)
    b = b.reshape(n_e, 4, n_grp, LRU_CH).transpose(0, 2, 1, 3).reshape(n_e, n_grp, 1, 4 * LRU_CH)
    return (0.5 * w).astype(BF16), 0.5 * b


def _lru(rest, e, conv_w, conv_b, w, b, lru_lambda):
    n_grp = LRU_WIDTH // LRU_CH
    xb0 = LRU_WIDTH // LRU_CH
    gb0 = 2 * LRU_WIDTH // LRU_CH
    return pl.pallas_call(
        _lru_kernel,
        grid=(BATCH, n_grp),
        in_specs=[
            pl.BlockSpec((1, SEQ, LRU_CH), lambda bb, c: (bb, 0, xb0 + c)),
            pl.BlockSpec((1, SEQ, LRU_CH), lambda bb, c: (bb, 0, gb0 + c)),
            pl.BlockSpec((1, 4, LRU_CH), lambda bb, c: (e, 0, c)),
            pl.BlockSpec((1, 1, LRU_CH), lambda bb, c: (e, 0, c)),
            pl.BlockSpec((1, 1, LRU_CH, 4 * LRU_CH), lambda bb, c: (e, c, 0, 0)),
            pl.BlockSpec((1, 1, 1, 4 * LRU_CH), lambda bb, c: (e, c, 0, 0)),
            pl.BlockSpec((1, 2, LRU_CH), lambda bb, c: (e, 0, c)),
        ],
        out_specs=pl.BlockSpec((1, SEQ, LRU_CH), lambda bb, c: (bb, 0, c)),
        out_shape=jax.ShapeDtypeStruct((BATCH, SEQ, LRU_WIDTH), BF16),
        scratch_shapes=[
            pltpu.VMEM((SEQ, LRU_CH), F32),
            pltpu.VMEM((2, SEQ // SUBLANE, SUBLANE, LRU_CH), F32),
            pltpu.VMEM((2, SEQ // SUBLANE, SUBLANE, LRU_CH), F32),
            pltpu.VMEM((2, SEQ // SUBLANE, 1, LRU_CH), F32),
        ],
        compiler_params=_cparams(),
        name="rglru",
    )(rest, rest, conv_w, conv_b.reshape(-1, 1, LRU_WIDTH), w, b, lru_lambda)


def _na_rpb_kernel(rpb_ref, dc_ref, o_ref):
    dc = dc_ref[...]
    n_dc = 2 * NA_WIN_C - 1
    val = jnp.broadcast_to(rpb_ref[:, 0:1], o_ref.shape)
    for j in range(1, n_dc):
        val = jnp.where(dc == j, rpb_ref[:, j:j + 1], val)
    o_ref[...] = val


def _na_bias(rpb):
    n_l = rpb.shape[0]
    n_dr = 2 * NA_WIN_R - 1
    n_dc = 2 * NA_WIN_C - 1
    cols = np.arange(GRID_W)
    dc_idx = (np.clip(cols[None, :] - cols[:, None], -(NA_WIN_C - 1), NA_WIN_C - 1)
              + NA_WIN_C - 1).astype(np.int32)
    toe = pl.pallas_call(
        _na_rpb_kernel,
        out_shape=jax.ShapeDtypeStruct((n_l * NA_HEADS * n_dr, GRID_W * GRID_W), F32),
        compiler_params=_cparams(),
        name="na_rpb",
    )(rpb.reshape(n_l * NA_HEADS * n_dr, n_dc), jnp.asarray(dc_idx.reshape(1, -1)))
    col_start = np.clip(cols - NA_WIN_C // 2, 0, GRID_W - NA_WIN_C)
    col_mask = (cols[None, :] >= col_start[:, None]) & (cols[None, :] < col_start[:, None] + NA_WIN_C)
    neg = jnp.asarray(np.where(col_mask, 0.0, -np.inf).astype(np.float32))
    toe = LOG2E * toe.reshape(n_l, NA_HEADS, n_dr, GRID_W, GRID_W) + neg
    tiles = jnp.concatenate([toe[:, :, :n_dr - 1], toe[:, :, 1:]], axis=-1)
    tiles = tiles.reshape(n_l, NA_HEADS // 2, 2, n_dr - 1, GRID_W, LANE)
    return tiles.transpose(0, 1, 3, 2, 4, 5).reshape(n_l, NA_HEADS // 2, n_dr - 1, 2 * GRID_W, LANE)


def _na_kernel(q_ref, k_ref, v_ref, g_ref, bias_ref, o_ref):
    win = NA_WIN_R * GRID_W
    lane = lax.broadcasted_iota(jnp.int32, (GRID_W, LANE), 1)
    lo = lane < NA_HEAD_DIM

    def group_body(gi, carry):
        q0s, vws, ss = [], [], []
        for i in range(NA_ROWS_PER_STEP):
            r = gi * NA_ROWS_PER_STEP + i
            rs = jnp.clip(r - NA_WIN_R // 2, 0, GRID_ROWS - NA_WIN_R)
            k0 = pl.multiple_of(rs * GRID_W, GRID_W)
            q0 = pl.multiple_of(r * GRID_W, GRID_W)
            q = q_ref[0, pl.ds(q0, GRID_W), :]
            zero = jnp.zeros_like(q)
            q2 = jnp.concatenate([jnp.where(lo, q, zero), jnp.where(lo, zero, q)], axis=0)
            kw = k_ref[0, pl.ds(k0, win), :]
            dr0 = NA_WIN_R - 1 - (r - rs)
            bias = jnp.concatenate([bias_ref[0, 0, dr0 + 2 * j] for j in range(NA_WIN_R // 2)], axis=1)
            ss.append(lax.dot_general(q2, kw, NT_DIMS, preferred_element_type=F32) + bias)
            vws.append(v_ref[0, pl.ds(k0, win), :])
            q0s.append(q0)
        es, rinv = [], []
        for s in ss:
            e = jnp.exp2(s - jnp.max(s, axis=-1, keepdims=True))
            rinv.append(1.0 / jnp.sum(e, axis=-1, keepdims=True))
            es.append(e.astype(BF16))
        pvs = [jnp.dot(e, vw, preferred_element_type=F32) for e, vw in zip(es, vws)]
        for pv, ri, q0 in zip(pvs, rinv, q0s):
            o2 = pv * ri
            o = jnp.where(lo, o2[:GRID_W], o2[GRID_W:])
            o_ref[0, pl.ds(q0, GRID_W), :] = (o * _silu(g_ref[0, pl.ds(q0, GRID_W), :])).astype(o_ref.dtype)
        return carry

    lax.fori_loop(0, GRID_ROWS // NA_ROWS_PER_STEP, group_body, 0)


def _na(qkv, gate, bias, o):
    n_pair = NA_HEADS // 2
    return pl.pallas_call(
        _na_kernel,
        grid=(BATCH, n_pair),
        in_specs=[
            pl.BlockSpec((1, SEQ, LANE), lambda b, p: (b, 0, p)),
            pl.BlockSpec((1, SEQ, LANE), lambda b, p: (b, 0, n_pair + p)),
            pl.BlockSpec((1, SEQ, LANE), lambda b, p: (b, 0, 2 * n_pair + p)),
            pl.BlockSpec((1, SEQ, LANE), lambda b, p: (b, 0, p)),
            pl.BlockSpec((1, 1, 2 * NA_WIN_R - 2, 2 * GRID_W, LANE), lambda b, p: (o, p, 0, 0, 0)),
        ],
        out_specs=pl.BlockSpec((1, SEQ, LANE), lambda b, p: (b, 0, p)),
        out_shape=jax.ShapeDtypeStruct((BATCH, SEQ, NA_WIDTH), BF16),
        compiler_params=_cparams(),
        name="nbr_attn",
    )(qkv, qkv, qkv, gate, bias)


def kernel(x, c, ada_w, ada_b, norm_g, final_g, t5_table, even_w_in, even_w_out, da_lam, da_subln_g,
           lru_conv_w, lru_conv_b, lru_w_a, lru_b_a, lru_w_x, lru_b_x, lru_lambda,
           odd_w_in, odd_w_out, na_rpb):
    assert x.shape == (BATCH, SEQ, D_MODEL) and x.dtype == F32 and c.shape == (BATCH, D_MODEL)
    assert even_w_in.shape == (DEPTH // 2, D_MODEL, 4 * DA_WIDTH + 2 * LRU_WIDTH)
    assert odd_w_in.shape == (DEPTH // 2, D_MODEL, 4 * NA_WIDTH)
    assert na_rpb.shape == (DEPTH // 2, NA_HEADS, 2 * NA_WIN_R - 1, 2 * NA_WIN_C - 1)
    mod = _ada_mod(c, ada_w, ada_b)
    tab_log2 = LOG2E * t5_table
    tiles = _t5_tiles(tab_log2)
    na_bias = _na_bias(na_rpb)
    lru_w, lru_b = _lru_weights(lru_w_a, lru_b_a, lru_w_x, lru_b_x)

    def w_in_bf16(w, n_q, head_dim):
        col_scale = np.ones((w.shape[-1],), np.float32)
        col_scale[:n_q] = head_dim ** -0.5 * LOG2E
        return (w * col_scale).astype(BF16)

    w_in = [w_in_bf16(even_w_in, DA_WIDTH, DA_HEAD_DIM), w_in_bf16(odd_w_in, NA_WIDTH, NA_HEAD_DIM)]
    w_out = [even_w_out.astype(BF16), odd_w_out.astype(BF16)]
    widths_in = [(3 * DA_WIDTH, DA_WIDTH + 2 * LRU_WIDTH), (3 * NA_WIDTH, NA_WIDTH)]

    def in_args(l):
        n1, n2 = widths_in[l % 2]
        return dict(w_in=w_in[l % 2], norm_g=norm_g, layer_in=l, idx_in=l // 2, n1=n1, n2=n2)

    act, rest = _proj(x, mod, **in_args(0))
    for l in range(DEPTH):
        i = l // 2
        if l % 2 == 0:
            lam_init = 0.8 - 0.6 * math.exp(-0.3 * l)
            mixed = [_dattn(act, rest, tiles, tab_log2, da_lam[i], da_subln_g[i], lam_init),
                     _lru(rest, i, lru_conv_w, lru_conv_b, lru_w, lru_b, lru_lambda)]
        else:
            mixed = [_na(act, rest, na_bias, i)]
        out_args = dict(mixed=mixed, w_out=w_out[l % 2], layer_out=l, idx_out=i)
        if l + 1 < DEPTH:
            x, act, rest = _proj(x, mod, **out_args, **in_args(l + 1))
        else:
            (x,) = _proj(x, mod, **out_args, final_g=final_g)
    return x
```

```python
import functools
import math

import numpy as np
import jax
import jax.numpy as jnp
from jax import lax
from jax.experimental import pallas as pl
from jax.experimental.pallas import tpu as pltpu

F32 = jnp.float32
BF16 = jnp.bfloat16

D_MODEL = 1024
BATCH = 16
SEQ = 2048
DEPTH = 4
GRID_W = 64
EPS = 1e-6

DA_HEADS = 4
DA_HEAD_DIM = 64
DA_WIDTH = DA_HEADS * 2 * DA_HEAD_DIM
T5_BUCKETS = 32
T5_MAX_DIST = 128
LRU_WIDTH = D_MODEL // 2
LRU_BLOCKS = 8
LRU_BLOCK = LRU_WIDTH // LRU_BLOCKS
LRU_C = 8.0
NA_HEADS = 16
NA_HEAD_DIM = 64
NA_WIDTH = NA_HEADS * NA_HEAD_DIM
NA_WIN_R = 8
NA_WIN_C = 16
GRID_ROWS = SEQ // GRID_W

LANE = 128
SUBLANE = 8
VMEM_LIMIT = 48 * 1024 * 1024

TM_PROJ = 512
TN_PROJ = 512
TQ_ATTN = 512
LRU_CH = 256
LRU_TC = 256
N_TILE_DIAG = 2 * (SEQ // LANE) - 1
T5_NEAR = 1
DA_STRIP = SUBLANE
LOG2E = math.log2(math.e)
NA_ROWS_PER_STEP = 16

NT_DIMS = (((1,), (1,)), ((), ()))


def _cparams():
    return pltpu.CompilerParams(vmem_limit_bytes=VMEM_LIMIT)


def _silu(x):
    return x * jax.nn.sigmoid(x)


def _ada_kernel(c_ref, w_ref, b_ref, o_ref):
    ca = _silu(c_ref[...])
    o_ref[0, 0] = jnp.dot(ca, w_ref[0], preferred_element_type=F32) + b_ref[0, 0]


def _ada_mod(c, ada_w, ada_b):
    return pl.pallas_call(
        _ada_kernel,
        grid=(DEPTH, 3),
        in_specs=[
            pl.BlockSpec((BATCH, D_MODEL), lambda l, j: (0, 0)),
            pl.BlockSpec((1, D_MODEL, D_MODEL), lambda l, j: (l, 0, j)),
            pl.BlockSpec((1, 1, 1, D_MODEL), lambda l, j: (l, j, 0, 0)),
        ],
        out_specs=pl.BlockSpec((1, 1, BATCH, D_MODEL), lambda l, j: (l, j, 0, 0)),
        out_shape=jax.ShapeDtypeStruct((DEPTH, 3, BATCH, D_MODEL), F32),
        compiler_params=_cparams(),
        name="ada_mod",
    )(c, ada_w, ada_b.reshape(DEPTH, 3, 1, D_MODEL))


def _rms(x, g):
    return x * lax.rsqrt(jnp.mean(x * x, axis=-1, keepdims=True) + EPS) * g


def _proj_kernel(*refs, widths, has_in, final, n1, n2, chunk):
    it = iter(refs)
    b = pl.program_id(0)
    m_refs = [next(it) for _ in widths]
    if widths:
        wo_ref, mod_ref = next(it), next(it)
    x_ref = next(it)
    if has_in:
        modn_ref, g_ref, wi_ref = next(it), next(it), next(it)
    if final:
        fg_ref = next(it)
    if widths:
        xo_ref = next(it)
    if has_in:
        o1_ref, o2_ref = next(it), next(it)

    x = x_ref[0]
    if widths:
        y, off = None, 0
        for m_ref, wd in zip(m_refs, widths):
            part = jnp.dot(m_ref[0], wo_ref[0, off:off + wd, :], preferred_element_type=F32)
            y = part if y is None else y + part
            off += wd
        x = x + mod_ref[0, 2, pl.ds(b, 1), :] * y
        xo_ref[0] = _rms(x, fg_ref[...]) if final else x
    if has_in:
        y = _rms(x, g_ref[0])
        h = (y * (1.0 + modn_ref[0, 1, pl.ds(b, 1), :]) + modn_ref[0, 0, pl.ds(b, 1), :]).astype(BF16)
        for j in range(0, n1, chunk):
            o1_ref[0, :, j:j + chunk] = jnp.dot(
                h, wi_ref[0, :, j:j + chunk], preferred_element_type=F32).astype(o1_ref.dtype)
        for j in range(0, n2, chunk):
            o2_ref[0, :, j:j + chunk] = jnp.dot(
                h, wi_ref[0, :, n1 + j:n1 + j + chunk], preferred_element_type=F32).astype(o2_ref.dtype)


def _proj(x, mod, *, mixed=(), w_out=None, layer_out=None, w_in=None, norm_g=None, layer_in=None,
          idx_out=0, idx_in=0, n1=0, n2=0, final_g=None):
    widths = tuple(m.shape[-1] for m in mixed)
    has_in = w_in is not None
    final = final_g is not None
    kern = functools.partial(_proj_kernel, widths=widths, has_in=has_in, final=final, n1=n1, n2=n2,
                             chunk=TN_PROJ)
    tile = lambda wd: pl.BlockSpec((1, TM_PROJ, wd), lambda b, i: (b, i, 0))
    const = lambda shape, first: pl.BlockSpec(shape, lambda b, i: (first,) + (0,) * (len(shape) - 1),
                                              pipeline_mode=pl.Buffered(1))
    mod_spec = lambda layer: const((1, 3, BATCH, D_MODEL), layer)
    in_specs, args = [tile(wd) for wd in widths], list(mixed)
    if widths:
        in_specs += [const((1, sum(widths), D_MODEL), idx_out), mod_spec(layer_out)]
        args += [w_out, mod]
    in_specs.append(tile(D_MODEL))
    args.append(x)
    if has_in:
        in_specs += [mod_spec(layer_in), const((1, 1, D_MODEL), layer_in), const((1, D_MODEL, n1 + n2), idx_in)]
        args += [mod, norm_g.reshape(DEPTH, 1, D_MODEL), w_in]
    if final:
        in_specs.append(const((1, D_MODEL), 0))
        args.append(final_g.reshape(1, D_MODEL))
    out_specs, out_shape = [], []
    if widths:
        out_specs.append(tile(D_MODEL))
        out_shape.append(jax.ShapeDtypeStruct((BATCH, SEQ, D_MODEL), F32))
    if has_in:
        out_specs += [tile(n1), tile(n2)]
        out_shape += [jax.ShapeDtypeStruct((BATCH, SEQ, n1), BF16), jax.ShapeDtypeStruct((BATCH, SEQ, n2), F32)]
    return pl.pallas_call(
        kern,
        grid=(BATCH, SEQ // TM_PROJ),
        in_specs=in_specs,
        out_specs=out_specs,
        out_shape=out_shape,
        compiler_params=_cparams(),
        name="proj_" + ("o" if widths else "") + ("i" if has_in else "") + ("f" if final else ""),
    )(*args)


def _t5_bucket_np(rel):
    nb = T5_BUCKETS // 2
    max_exact = nb // 2
    ret = np.where(rel > 0, nb, 0)
    n = np.abs(rel)
    nf = np.maximum(n, 1).astype(np.float32)
    large = max_exact + (np.log(nf / np.float32(max_exact)) / np.float32(math.log(T5_MAX_DIST / max_exact))
                         * np.float32(nb - max_exact)).astype(np.int32)
    large = np.minimum(large, nb - 1)
    return (ret + np.where(n < max_exact, n, large)).astype(np.int32)


def _t5_tile_buckets():
    d = np.arange(N_TILE_DIAG)[:, None, None] - (SEQ // LANE - 1)
    i = np.arange(LANE)[None, :, None]
    j = np.arange(LANE)[None, None, :]
    idx = _t5_bucket_np(d * LANE + j - i)
    mid = SEQ // LANE - 1
    assert (idx[:mid - T5_NEAR] == T5_BUCKETS // 2 - 1).all() and (idx[mid + T5_NEAR + 1:] == T5_BUCKETS - 1).all()
    return idx[mid - T5_NEAR:mid + T5_NEAR + 1]


def _t5_tiles_kernel(tab_ref, idx_ref, o_ref):
    h = pl.program_id(0)
    for d in range(2 * T5_NEAR + 1):
        idx = idx_ref[d]
        val = jnp.full(idx.shape, tab_ref[h, 0], F32)
        for j in range(1, T5_BUCKETS):
            val = jnp.where(idx == j, tab_ref[h, j], val)
        o_ref[0, d] = val


def _t5_tiles(t5_table):
    idx = jnp.asarray(_t5_tile_buckets())
    n_near = 2 * T5_NEAR + 1
    return pl.pallas_call(
        _t5_tiles_kernel,
        grid=(DA_HEADS,),
        in_specs=[
            pl.BlockSpec(memory_space=pltpu.SMEM),
            pl.BlockSpec((n_near, LANE, LANE), lambda h: (0, 0, 0)),
        ],
        out_specs=pl.BlockSpec((1, n_near, LANE, LANE), lambda h: (h, 0, 0, 0)),
        out_shape=jax.ShapeDtypeStruct((DA_HEADS, n_near, LANE, LANE), F32),
        compiler_params=_cparams(),
        name="t5_tiles",
    )(t5_table, idx)


def _dattn_kernel(tab_ref, lam_ref, q_ref, k_ref, v_ref, t_ref, ga_ref, sg_ref, o_ref,
                  s_ref, a_ref, r_ref, *, lam_init):
    h = pl.program_id(0)
    n_kb = SEQ // LANE
    n_tiles = SEQ // TQ_ATTN
    c_left = tab_ref[h, T5_BUCKETS // 2 - 1]
    c_right = tab_ref[h, T5_BUCKETS - 1]

    lp = lam_ref[...]
    lam = (jnp.exp(jnp.sum(lp[0:1] * lp[1:2], axis=-1, keepdims=True))
           - jnp.exp(jnp.sum(lp[2:3] * lp[3:4], axis=-1, keepdims=True)) + lam_init)
    lane = lax.broadcasted_iota(jnp.int32, (TQ_ATTN, LANE), 1)

    def phase_a(i, slot):
        q = q_ref[0, i * TQ_ATTN:(i + 1) * TQ_ATTN, :]
        zero = jnp.zeros_like(q)
        qst = jnp.concatenate([jnp.where(lane < DA_HEAD_DIM, q, zero),
                               jnp.where(lane >= DA_HEAD_DIM, q, zero)], axis=0)
        s_ref[slot] = lax.dot_general(qst, k_ref[0], NT_DIMS, preferred_element_type=F32)

    def strip_softmax(slot, row0, qb, rr):
        cols, m_left, m_right, m_near = [], None, None, None
        for kb in range(n_kb):
            x = s_ref[slot, row0:row0 + DA_STRIP, kb * LANE:(kb + 1) * LANE]
            d = kb - qb
            if abs(d) <= T5_NEAR:
                x = x + t_ref[0, d + T5_NEAR, rr:rr + DA_STRIP, :]
                m_near = x if m_near is None else jnp.maximum(m_near, x)
            elif d < 0:
                m_left = x if m_left is None else jnp.maximum(m_left, x)
            else:
                m_right = x if m_right is None else jnp.maximum(m_right, x)
            cols.append(x)
        m = m_near
        if m_left is not None:
            m = jnp.maximum(m, m_left + c_left)
        if m_right is not None:
            m = jnp.maximum(m, m_right + c_right)
        m = jnp.max(m, axis=-1, keepdims=True)
        off_left, off_right = m - c_left, m - c_right
        es, tot = [], None
        for kb in range(n_kb):
            d = kb - qb
            off = m if abs(d) <= T5_NEAR else (off_left if d < 0 else off_right)
            e = jnp.exp2(cols[kb] - off)
            tot = e if tot is None else tot + e
            es.append(e)
        return es, jnp.sum(tot, axis=-1, keepdims=True)

    def strip_weights(i, slot, r0):
        qb = (i * TQ_ATTN + r0) // LANE
        rr = (i * TQ_ATTN + r0) % LANE
        e1, l1 = strip_softmax(slot, r0, qb, rr)
        e2, l2 = strip_softmax(slot, TQ_ATTN + r0, qb, rr)
        cc = lam * l1 * (1.0 / l2)
        r_ref[slot, r0:r0 + DA_STRIP, :] = jnp.broadcast_to(1.0 / l1, (DA_STRIP, LANE))
        return [x1 - cc * x2 for x1, x2 in zip(e1, e2)]

    def phase_b(i, slot):
        for r0 in range(0, TQ_ATTN, 2 * DA_STRIP):
            lo = strip_weights(i, slot, r0)
            hi = strip_weights(i, slot, r0 + DA_STRIP)
            for kb in range(n_kb):
                a_ref[slot, r0:r0 + 2 * DA_STRIP, kb * LANE:(kb + 1) * LANE] = (
                    jnp.concatenate([lo[kb], hi[kb]], axis=0).astype(BF16))

    def phase_c(i, slot):
        o = jnp.dot(a_ref[slot], v_ref[0], preferred_element_type=F32) * r_ref[slot]
        ms = jnp.mean(o * o, axis=-1, keepdims=True)
        o = o * lax.rsqrt(ms + EPS) * sg_ref[...] * (1.0 - lam_init)
        rows = slice(i * TQ_ATTN, (i + 1) * TQ_ATTN)
        o_ref[0, rows, :] = (o * _silu(ga_ref[0, rows, :])).astype(o_ref.dtype)

    phase_a(0, 0)
    for i in range(n_tiles):
        if i + 1 < n_tiles:
            phase_a(i + 1, (i + 1) % 2)
        phase_b(i, i % 2)
        phase_c(i, i % 2)


def _dattn(qkv, rest, tiles, tab_log2, lam_p, subln_g, lam_init):
    kern = functools.partial(_dattn_kernel, lam_init=lam_init)
    nh = DA_HEADS
    return pl.pallas_call(
        kern,
        grid=(nh, BATCH),
        in_specs=[
            pl.BlockSpec(memory_space=pltpu.SMEM),
            pl.BlockSpec((4, DA_HEAD_DIM), lambda h, b: (0, 0)),
            pl.BlockSpec((1, SEQ, LANE), lambda h, b: (b, 0, h)),
            pl.BlockSpec((1, SEQ, LANE), lambda h, b: (b, 0, nh + h)),
            pl.BlockSpec((1, SEQ, LANE), lambda h, b: (b, 0, 2 * nh + h)),
            pl.BlockSpec((1, 2 * T5_NEAR + 1, LANE, LANE), lambda h, b: (h, 0, 0, 0)),
            pl.BlockSpec((1, SEQ, LANE), lambda h, b: (b, 0, h)),
            pl.BlockSpec((1, LANE), lambda h, b: (0, 0)),
        ],
        out_specs=pl.BlockSpec((1, SEQ, LANE), lambda h, b: (b, 0, h)),
        out_shape=jax.ShapeDtypeStruct((BATCH, SEQ, DA_WIDTH), BF16),
        scratch_shapes=[
            pltpu.VMEM((2, 2 * TQ_ATTN, SEQ), F32),
            pltpu.VMEM((2, TQ_ATTN, SEQ), BF16),
            pltpu.VMEM((2, TQ_ATTN, LANE), F32),
        ],
        compiler_params=_cparams(),
        name="diff_attn",
    )(tab_log2, lam_p, qkv, qkv, qkv, tiles, rest, subln_g.reshape(1, LANE))


def _lru_kernel(x_ref, g_ref, cw_ref, cb_ref, w_ref, b_ref, lam_ref, o_ref,
                xc_ref, a3, u3, car_ref):
    S, C = SEQ, LRU_CH
    n_tiles = S // SUBLANE
    tiles_per_chunk = LRU_TC // SUBLANE

    x = x_ref[0]
    row = lax.broadcasted_iota(jnp.int32, (S, C), 0)
    xm2 = jnp.where(row >= 2, pltpu.roll(x, 2, 0), 0.0)
    xm1 = jnp.where(row >= 1, pltpu.roll(x, 1, 0), 0.0)
    xp1 = jnp.where(row < S - 1, pltpu.roll(x, S - 1, 0), 0.0)
    xc_ref[...] = (cw_ref[0, 0:1, :] * xm2 + cw_ref[0, 1:2, :] * xm1 + cw_ref[0, 2:3, :] * x
                   + cw_ref[0, 3:4, :] * xp1 + cb_ref[0])

    row8 = lax.broadcasted_iota(jnp.int32, (tiles_per_chunk, SUBLANE, C), 1)
    lam = lam_ref[0]
    half_c = (-0.5 * LRU_C) * (jnp.maximum(-lam, 0.0) + jnp.log1p(jnp.exp(-jnp.abs(lam))))

    def chunk_body(ci, carry):
        t0 = pl.multiple_of(ci * LRU_TC, LRU_TC)
        j0 = pl.multiple_of(ci * tiles_per_chunk, tiles_per_chunk)
        xc = xc_ref[pl.ds(t0, LRU_TC), :]
        xch = 0.5 * xc
        zh = jnp.dot(xc.astype(BF16), w_ref[0, 0], preferred_element_type=F32) + b_ref[0, 0]
        for direction in range(2):
            hc = half_c[direction:direction + 1, :]
            log_a = hc * jnp.tanh(zh[:, 2 * direction * C:(2 * direction + 1) * C]) + hc
            a = jnp.exp(log_a)
            m2 = -jnp.tanh(log_a) * (a * a + 1.0)
            mult = jnp.where(m2 > 0.0, m2 * lax.rsqrt(m2), 0.0)
            u = mult * (jnp.tanh(zh[:, (2 * direction + 1) * C:(2 * direction + 2) * C]) + 1.0) * xch
            a = a.reshape(tiles_per_chunk, SUBLANE, C)
            u = u.reshape(tiles_per_chunk, SUBLANE, C)
            for d in (1, 2, 4):
                if direction == 1:
                    ar = pltpu.roll(a, SUBLANE - d, 1)
                    ur = pltpu.roll(u, SUBLANE - d, 1)
                    m = row8 < SUBLANE - d
                else:
                    ar = pltpu.roll(a, d, 1)
                    ur = pltpu.roll(u, d, 1)
                    m = row8 >= d
                u = jnp.where(m, a * ur + u, u)
                a = jnp.where(m, a * ar, a)
            a3[direction, pl.ds(j0, tiles_per_chunk)] = a
            u3[direction, pl.ds(j0, tiles_per_chunk)] = u
        return carry

    lax.fori_loop(0, S // LRU_TC, chunk_body, 0)

    def carry_body(jf, c):
        cf, cb = c
        jb = n_tiles - 1 - jf
        car_ref[0, jf] = cf
        car_ref[1, jb] = cb
        last = SUBLANE - 1
        return (u3[0, jf, last:, :] + a3[0, jf, last:, :] * cf,
                u3[1, jb, 0:1, :] + a3[1, jb, 0:1, :] * cb)

    zero = jnp.zeros((1, C), F32)
    lax.fori_loop(0, n_tiles, carry_body, (zero, zero), unroll=8)

    hsum = (u3[0] + a3[0] * car_ref[0]) + (u3[1] + a3[1] * car_ref[1])
    o_ref[0] = (hsum.reshape(S, C) * _silu(g_ref[0])).astype(o_ref.dtype)


def _lru_weights(w_a, b_a, w_x, b_x):
    n_e = w_a.shape[0]
    n_grp = LRU_WIDTH // LRU_CH
    bpg = LRU_CH // LRU_BLOCK
    gates = jnp.stack([w_a[:, 0], w_x[:, 0], w_a[:, 1], w_x[:, 1]], axis=1)
    gates = gates.reshape(n_e, 4, n_grp, bpg, LRU_BLOCK, LRU_BLOCK)
    w = jnp.einsum('eqgncd,nm->egncqmd', gates, jnp.eye(bpg, dtype=gates.dtype))
    w = w.reshape(n_e, n_grp, LRU_CH, 4 * LRU_CH)
    b = jnp.stack([b_a[:, 0], b_x[:, 0], b_a[:, 1], b_x[:, 1]], axis=1)
    b = b.reshape(n_e, 4, n_grp, LRU_CH).transpose(0, 2, 1, 3).reshape(n_e, n_grp, 1, 4 * LRU_CH)
    return (0.5 * w).astype(BF16), 0.5 * b


def _lru(rest, e, conv_w, conv_b, w, b, lru_lambda):
    n_grp = LRU_WIDTH // LRU_CH
    xb0 = LRU_WIDTH // LRU_CH
    gb0 = 2 * LRU_WIDTH // LRU_CH
    return pl.pallas_call(
        _lru_kernel,
        grid=(BATCH, n_grp),
        in_specs=[
            pl.BlockSpec((1, SEQ, LRU_CH), lambda bb, c: (bb, 0, xb0 + c)),
            pl.BlockSpec((1, SEQ, LRU_CH), lambda bb, c: (bb, 0, gb0 + c)),
            pl.BlockSpec((1, 4, LRU_CH), lambda bb, c: (e, 0, c)),
            pl.BlockSpec((1, 1, LRU_CH), lambda bb, c: (e, 0, c)),
            pl.BlockSpec((1, 1, LRU_CH, 4 * LRU_CH), lambda bb, c: (e, c, 0, 0)),
            pl.BlockSpec((1, 1, 1, 4 * LRU_CH), lambda bb, c: (e, c, 0, 0)),
            pl.BlockSpec((1, 2, LRU_CH), lambda bb, c: (e, 0, c)),
        ],
        out_specs=pl.BlockSpec((1, SEQ, LRU_CH), lambda bb, c: (bb, 0, c)),
        out_shape=jax.ShapeDtypeStruct((BATCH, SEQ, LRU_WIDTH), BF16),
        scratch_shapes=[
            pltpu.VMEM((SEQ, LRU_CH), F32),
            pltpu.VMEM((2, SEQ // SUBLANE, SUBLANE, LRU_CH), F32),
            pltpu.VMEM((2, SEQ // SUBLANE, SUBLANE, LRU_CH), F32),
            pltpu.VMEM((2, SEQ // SUBLANE, 1, LRU_CH), F32),
        ],
        compiler_params=_cparams(),
        name="rglru",
    )(rest, rest, conv_w, conv_b.reshape(-1, 1, LRU_WIDTH), w, b, lru_lambda)


def _na_rpb_kernel(rpb_ref, dc_ref, o_ref):
    dc = dc_ref[...]
    n_dc = 2 * NA_WIN_C - 1
    val = jnp.broadcast_to(rpb_ref[:, 0:1], o_ref.shape)
    for j in range(1, n_dc):
        val = jnp.where(dc == j, rpb_ref[:, j:j + 1], val)
    o_ref[...] = val


def _na_bias(rpb):
    n_l = rpb.shape[0]
    n_dr = 2 * NA_WIN_R - 1
    n_dc = 2 * NA_WIN_C - 1
    cols = np.arange(GRID_W)
    dc_idx = (np.clip(cols[None, :] - cols[:, None], -(NA_WIN_C - 1), NA_WIN_C - 1)
              + NA_WIN_C - 1).astype(np.int32)
    toe = pl.pallas_call(
        _na_rpb_kernel,
        out_shape=jax.ShapeDtypeStruct((n_l * NA_HEADS * n_dr, GRID_W * GRID_W), F32),
        compiler_params=_cparams(),
        name="na_rpb",
    )(rpb.reshape(n_l * NA_HEADS * n_dr, n_dc), jnp.asarray(dc_idx.reshape(1, -1)))
    col_start = np.clip(cols - NA_WIN_C // 2, 0, GRID_W - NA_WIN_C)
    col_mask = (cols[None, :] >= col_start[:, None]) & (cols[None, :] < col_start[:, None] + NA_WIN_C)
    neg = jnp.asarray(np.where(col_mask, 0.0, -np.inf).astype(np.float32))
    toe = LOG2E * toe.reshape(n_l, NA_HEADS, n_dr, GRID_W, GRID_W) + neg
    tiles = jnp.concatenate([toe[:, :, :n_dr - 1], toe[:, :, 1:]], axis=-1)
    tiles = tiles.reshape(n_l, NA_HEADS // 2, 2, n_dr - 1, GRID_W, LANE)
    return tiles.transpose(0, 1, 3, 2, 4, 5).reshape(n_l, NA_HEADS // 2, n_dr - 1, 2 * GRID_W, LANE)


def _na_kernel(q_ref, k_ref, v_ref, g_ref, bias_ref, o_ref):
    win = NA_WIN_R * GRID_W
    lane = lax.broadcasted_iota(jnp.int32, (GRID_W, LANE), 1)
    lo = lane < NA_HEAD_DIM

    def group_body(gi, carry):
        q0s, vws, ss = [], [], []
        for i in range(NA_ROWS_PER_STEP):
            r = gi * NA_ROWS_PER_STEP + i
            rs = jnp.clip(r - NA_WIN_R // 2, 0, GRID_ROWS - NA_WIN_R)
            k0 = pl.multiple_of(rs * GRID_W, GRID_W)
            q0 = pl.multiple_of(r * GRID_W, GRID_W)
            q = q_ref[0, pl.ds(q0, GRID_W), :]
            zero = jnp.zeros_like(q)
            q2 = jnp.concatenate([jnp.where(lo, q, zero), jnp.where(lo, zero, q)], axis=0)
            kw = k_ref[0, pl.ds(k0, win), :]
            dr0 = NA_WIN_R - 1 - (r - rs)
            bias = jnp.concatenate([bias_ref[0, 0, dr0 + 2 * j] for j in range(NA_WIN_R // 2)], axis=1)
            ss.append(lax.dot_general(q2, kw, NT_DIMS, preferred_element_type=F32) + bias)
            vws.append(v_ref[0, pl.ds(k0, win), :])
            q0s.append(q0)
        es, rinv = [], []
        for s in ss:
            e = jnp.exp2(s - jnp.max(s, axis=-1, keepdims=True))
            rinv.append(1.0 / jnp.sum(e, axis=-1, keepdims=True))
            es.append(e.astype(BF16))
        pvs = [jnp.dot(e, vw, preferred_element_type=F32) for e, vw in zip(es, vws)]
        for pv, ri, q0 in zip(pvs, rinv, q0s):
            o2 = pv * ri
            o = jnp.where(lo, o2[:GRID_W], o2[GRID_W:])
            o_ref[0, pl.ds(q0, GRID_W), :] = (o * _silu(g_ref[0, pl.ds(q0, GRID_W), :])).astype(o_ref.dtype)
        return carry

    lax.fori_loop(0, GRID_ROWS // NA_ROWS_PER_STEP, group_body, 0)


def _na(qkv, gate, bias, o):
    n_pair = NA_HEADS // 2
    return pl.pallas_call(
        _na_kernel,
        grid=(BATCH, n_pair),
        in_specs=[
            pl.BlockSpec((1, SEQ, LANE), lambda b, p: (b, 0, p)),
            pl.BlockSpec((1, SEQ, LANE), lambda b, p: (b, 0, n_pair + p)),
            pl.BlockSpec((1, SEQ, LANE), lambda b, p: (b, 0, 2 * n_pair + p)),
            pl.BlockSpec((1, SEQ, LANE), lambda b, p: (b, 0, p)),
            pl.BlockSpec((1, 1, 2 * NA_WIN_R - 2, 2 * GRID_W, LANE), lambda b, p: (o, p, 0, 0, 0)),
        ],
        out_specs=pl.BlockSpec((1, SEQ, LANE), lambda b, p: (b, 0, p)),
        out_shape=jax.ShapeDtypeStruct((BATCH, SEQ, NA_WIDTH), BF16),
        compiler_params=_cparams(),
        name="nbr_attn",
    )(qkv, qkv, qkv, gate, bias)


def kernel(x, c, ada_w, ada_b, norm_g, final_g, t5_table, even_w_in, even_w_out, da_lam, da_subln_g,
           lru_conv_w, lru_conv_b, lru_w_a, lru_b_a, lru_w_x, lru_b_x, lru_lambda,
           odd_w_in, odd_w_out, na_rpb):
    assert x.shape == (BATCH, SEQ, D_MODEL) and x.dtype == F32 and c.shape == (BATCH, D_MODEL)
    assert even_w_in.shape == (DEPTH // 2, D_MODEL, 4 * DA_WIDTH + 2 * LRU_WIDTH)
    assert odd_w_in.shape == (DEPTH // 2, D_MODEL, 4 * NA_WIDTH)
    assert na_rpb.shape == (DEPTH // 2, NA_HEADS, 2 * NA_WIN_R - 1, 2 * NA_WIN_C - 1)
    mod = _ada_mod(c, ada_w, ada_b)
    tab_log2 = LOG2E * t5_table
    tiles = _t5_tiles(tab_log2)
    na_bias = _na_bias(na_rpb)
    lru_w, lru_b = _lru_weights(lru_w_a, lru_b_a, lru_w_x, lru_b_x)

    def w_in_bf16(w, n_q, head_dim):
        col_scale = np.ones((w.shape[-1],), np.float32)
        col_scale[:n_q] = head_dim ** -0.5 * LOG2E
        return (w * col_scale).astype(BF16)

    w_in = [w_in_bf16(even_w_in, DA_WIDTH, DA_HEAD_DIM), w_in_bf16(odd_w_in, NA_WIDTH, NA_HEAD_DIM)]
    w_out = [even_w_out.astype(BF16), odd_w_out.astype(BF16)]
    widths_in = [(3 * DA_WIDTH, DA_WIDTH + 2 * LRU_WIDTH), (3 * NA_WIDTH, NA_WIDTH)]

    def in_args(l):
        n1, n2 = widths_in[l % 2]
        return dict(w_in=w_in[l % 2], norm_g=norm_g, layer_in=l, idx_in=l // 2, n1=n1, n2=n2)

    act, rest = _proj(x, mod, **in_args(0))
    for l in range(DEPTH):
        i = l // 2
        if l % 2 == 0:
            lam_init = 0.8 - 0.6 * math.exp(-0.3 * l)
            mixed = [_dattn(act, rest, tiles, tab_log2, da_lam[i], da_subln_g[i], lam_init),
                     _lru(rest, i, lru_conv_w, lru_conv_b, lru_w, lru_b, lru_lambda)]
        else:
            mixed = [_na(act, rest, na_bias, i)]
        out_args = dict(mixed=mixed, w_out=w_out[l % 2], layer_out=l, idx_out=i)
        if l + 1 < DEPTH:
            x, act, rest = _proj(x, mod, **out_args, **in_args(l + 1))
        else:
            (x,) = _proj(x, mod, **out_args, final_g=final_g)
    return x
```

```python
import functools
import math

import numpy as np
import jax
import jax.numpy as jnp
from jax import lax
from jax.experimental import pallas as pl
from jax.experimental.pallas import tpu as pltpu

F32 = jnp.float32
BF16 = jnp.bfloat16

D_MODEL = 1024
BATCH = 16
SEQ = 2048
DEPTH = 4
GRID_W = 64
EPS = 1e-6

DA_HEADS = 4
DA_HEAD_DIM = 64
DA_WIDTH = DA_HEADS * 2 * DA_HEAD_DIM
T5_BUCKETS = 32
T5_MAX_DIST = 128
LRU_WIDTH = D_MODEL // 2
LRU_BLOCKS = 8
LRU_BLOCK = LRU_WIDTH // LRU_BLOCKS
LRU_C = 8.0
NA_HEADS = 16
NA_HEAD_DIM = 64
NA_WIDTH = NA_HEADS * NA_HEAD_DIM
NA_WIN_R = 8
NA_WIN_C = 16
GRID_ROWS = SEQ // GRID_W

LANE = 128
SUBLANE = 8
VMEM_LIMIT = 48 * 1024 * 1024

TM_PROJ = 512
TN_PROJ = 512
TQ_ATTN = 512
LRU_CH = 256
LRU_TC = 512
N_TILE_DIAG = 2 * (SEQ // LANE) - 1
T5_NEAR = 1
DA_STRIP = SUBLANE
LOG2E = math.log2(math.e)
NA_ROWS_PER_STEP = 16

NT_DIMS = (((1,), (1,)), ((), ()))


def _cparams():
    return pltpu.CompilerParams(vmem_limit_bytes=VMEM_LIMIT)


def _silu(x):
    return x * jax.nn.sigmoid(x)


def _ada_kernel(c_ref, w_ref, b_ref, o_ref):
    ca = _silu(c_ref[...])
    o_ref[0, 0] = jnp.dot(ca, w_ref[0], preferred_element_type=F32) + b_ref[0, 0]


def _ada_mod(c, ada_w, ada_b):
    return pl.pallas_call(
        _ada_kernel,
        grid=(DEPTH, 3),
        in_specs=[
            pl.BlockSpec((BATCH, D_MODEL), lambda l, j: (0, 0)),
            pl.BlockSpec((1, D_MODEL, D_MODEL), lambda l, j: (l, 0, j)),
            pl.BlockSpec((1, 1, 1, D_MODEL), lambda l, j: (l, j, 0, 0)),
        ],
        out_specs=pl.BlockSpec((1, 1, BATCH, D_MODEL), lambda l, j: (l, j, 0, 0)),
        out_shape=jax.ShapeDtypeStruct((DEPTH, 3, BATCH, D_MODEL), F32),
        compiler_params=_cparams(),
        name="ada_mod",
    )(c, ada_w, ada_b.reshape(DEPTH, 3, 1, D_MODEL))


def _rms(x, g):
    return x * lax.rsqrt(jnp.mean(x * x, axis=-1, keepdims=True) + EPS) * g


def _proj_kernel(*refs, widths, has_in, final, n1, n2, chunk):
    it = iter(refs)
    b = pl.program_id(0)
    m_refs = [next(it) for _ in widths]
    if widths:
        wo_ref, mod_ref = next(it), next(it)
    x_ref = next(it)
    if has_in:
        modn_ref, g_ref, wi_ref = next(it), next(it), next(it)
    if final:
        fg_ref = next(it)
    if widths:
        xo_ref = next(it)
    if has_in:
        o1_ref, o2_ref = next(it), next(it)

    x = x_ref[0]
    if widths:
        y, off = None, 0
        for m_ref, wd in zip(m_refs, widths):
            part = jnp.dot(m_ref[0], wo_ref[0, off:off + wd, :], preferred_element_type=F32)
            y = part if y is None else y + part
            off += wd
        x = x + mod_ref[0, 2, pl.ds(b, 1), :] * y
        xo_ref[0] = _rms(x, fg_ref[...]) if final else x
    if has_in:
        y = _rms(x, g_ref[0])
        h = (y * (1.0 + modn_ref[0, 1, pl.ds(b, 1), :]) + modn_ref[0, 0, pl.ds(b, 1), :]).astype(BF16)
        for j in range(0, n1, chunk):
            o1_ref[0, :, j:j + chunk] = jnp.dot(
                h, wi_ref[0, :, j:j + chunk], preferred_element_type=F32).astype(o1_ref.dtype)
        for j in range(0, n2, chunk):
            o2_ref[0, :, j:j + chunk] = jnp.dot(
                h, wi_ref[0, :, n1 + j:n1 + j + chunk], preferred_element_type=F32).astype(o2_ref.dtype)


def _proj(x, mod, *, mixed=(), w_out=None, layer_out=None, w_in=None, norm_g=None, layer_in=None,
          idx_out=0, idx_in=0, n1=0, n2=0, final_g=None):
    widths = tuple(m.shape[-1] for m in mixed)
    has_in = w_in is not None
    final = final_g is not None
    kern = functools.partial(_proj_kernel, widths=widths, has_in=has_in, final=final, n1=n1, n2=n2,
                             chunk=TN_PROJ)
    tile = lambda wd: pl.BlockSpec((1, TM_PROJ, wd), lambda b, i: (b, i, 0))
    const = lambda shape, first: pl.BlockSpec(shape, lambda b, i: (first,) + (0,) * (len(shape) - 1),
                                              pipeline_mode=pl.Buffered(1))
    mod_spec = lambda layer: const((1, 3, BATCH, D_MODEL), layer)
    in_specs, args = [tile(wd) for wd in widths], list(mixed)
    if widths:
        in_specs += [const((1, sum(widths), D_MODEL), idx_out), mod_spec(layer_out)]
        args += [w_out, mod]
    in_specs.append(tile(D_MODEL))
    args.append(x)
    if has_in:
        in_specs += [mod_spec(layer_in), const((1, 1, D_MODEL), layer_in), const((1, D_MODEL, n1 + n2), idx_in)]
        args += [mod, norm_g.reshape(DEPTH, 1, D_MODEL), w_in]
    if final:
        in_specs.append(const((1, D_MODEL), 0))
        args.append(final_g.reshape(1, D_MODEL))
    out_specs, out_shape = [], []
    if widths:
        out_specs.append(tile(D_MODEL))
        out_shape.append(jax.ShapeDtypeStruct((BATCH, SEQ, D_MODEL), F32))
    if has_in:
        out_specs += [tile(n1), tile(n2)]
        out_shape += [jax.ShapeDtypeStruct((BATCH, SEQ, n1), BF16), jax.ShapeDtypeStruct((BATCH, SEQ, n2), F32)]
    return pl.pallas_call(
        kern,
        grid=(BATCH, SEQ // TM_PROJ),
        in_specs=in_specs,
        out_specs=out_specs,
        out_shape=out_shape,
        compiler_params=_cparams(),
        name="proj_" + ("o" if widths else "") + ("i" if has_in else "") + ("f" if final else ""),
    )(*args)


def _t5_bucket_np(rel):
    nb = T5_BUCKETS // 2
    max_exact = nb // 2
    ret = np.where(rel > 0, nb, 0)
    n = np.abs(rel)
    nf = np.maximum(n, 1).astype(np.float32)
    large = max_exact + (np.log(nf / np.float32(max_exact)) / np.float32(math.log(T5_MAX_DIST / max_exact))
                         * np.float32(nb - max_exact)).astype(np.int32)
    large = np.minimum(large, nb - 1)
    return (ret + np.where(n < max_exact, n, large)).astype(np.int32)


def _t5_tile_buckets():
    d = np.arange(N_TILE_DIAG)[:, None, None] - (SEQ // LANE - 1)
    i = np.arange(LANE)[None, :, None]
    j = np.arange(LANE)[None, None, :]
    idx = _t5_bucket_np(d * LANE + j - i)
    mid = SEQ // LANE - 1
    assert (idx[:mid - T5_NEAR] == T5_BUCKETS // 2 - 1).all() and (idx[mid + T5_NEAR + 1:] == T5_BUCKETS - 1).all()
    return idx[mid - T5_NEAR:mid + T5_NEAR + 1]


def _t5_tiles_kernel(tab_ref, idx_ref, o_ref):
    h = pl.program_id(0)
    for d in range(2 * T5_NEAR + 1):
        idx = idx_ref[d]
        val = jnp.full(idx.shape, tab_ref[h, 0], F32)
        for j in range(1, T5_BUCKETS):
            val = jnp.where(idx == j, tab_ref[h, j], val)
        o_ref[0, d] = val


def _t5_tiles(t5_table):
    idx = jnp.asarray(_t5_tile_buckets())
    n_near = 2 * T5_NEAR + 1
    return pl.pallas_call(
        _t5_tiles_kernel,
        grid=(DA_HEADS,),
        in_specs=[
            pl.BlockSpec(memory_space=pltpu.SMEM),
            pl.BlockSpec((n_near, LANE, LANE), lambda h: (0, 0, 0)),
        ],
        out_specs=pl.BlockSpec((1, n_near, LANE, LANE), lambda h: (h, 0, 0, 0)),
        out_shape=jax.ShapeDtypeStruct((DA_HEADS, n_near, LANE, LANE), F32),
        compiler_params=_cparams(),
        name="t5_tiles",
    )(t5_table, idx)


def _dattn_kernel(tab_ref, lam_ref, q_ref, k_ref, v_ref, t_ref, ga_ref, sg_ref, o_ref,
                  s_ref, a_ref, r_ref, *, lam_init):
    h = pl.program_id(0)
    n_kb = SEQ // LANE
    n_tiles = SEQ // TQ_ATTN
    c_left = tab_ref[h, T5_BUCKETS // 2 - 1]
    c_right = tab_ref[h, T5_BUCKETS - 1]

    lp = lam_ref[...]
    lam = (jnp.exp(jnp.sum(lp[0:1] * lp[1:2], axis=-1, keepdims=True))
           - jnp.exp(jnp.sum(lp[2:3] * lp[3:4], axis=-1, keepdims=True)) + lam_init)
    lane = lax.broadcasted_iota(jnp.int32, (TQ_ATTN, LANE), 1)

    def phase_a(i, slot):
        q = q_ref[0, i * TQ_ATTN:(i + 1) * TQ_ATTN, :]
        zero = jnp.zeros_like(q)
        qst = jnp.concatenate([jnp.where(lane < DA_HEAD_DIM, q, zero),
                               jnp.where(lane >= DA_HEAD_DIM, q, zero)], axis=0)
        s_ref[slot] = lax.dot_general(qst, k_ref[0], NT_DIMS, preferred_element_type=F32)

    def strip_softmax(slot, row0, qb, rr):
        cols, m_left, m_right, m_near = [], None, None, None
        for kb in range(n_kb):
            x = s_ref[slot, row0:row0 + DA_STRIP, kb * LANE:(kb + 1) * LANE]
            d = kb - qb
            if abs(d) <= T5_NEAR:
                x = x + t_ref[0, d + T5_NEAR, rr:rr + DA_STRIP, :]
                m_near = x if m_near is None else jnp.maximum(m_near, x)
            elif d < 0:
                m_left = x if m_left is None else jnp.maximum(m_left, x)
            else:
                m_right = x if m_right is None else jnp.maximum(m_right, x)
            cols.append(x)
        m = m_near
        if m_left is not None:
            m = jnp.maximum(m, m_left + c_left)
        if m_right is not None:
            m = jnp.maximum(m, m_right + c_right)
        m = jnp.max(m, axis=-1, keepdims=True)
        off_left, off_right = m - c_left, m - c_right
        es, tot = [], None
        for kb in range(n_kb):
            d = kb - qb
            off = m if abs(d) <= T5_NEAR else (off_left if d < 0 else off_right)
            e = jnp.exp2(cols[kb] - off)
            tot = e if tot is None else tot + e
            es.append(e)
        return es, jnp.sum(tot, axis=-1, keepdims=True)

    def strip_weights(i, slot, r0):
        qb = (i * TQ_ATTN + r0) // LANE
        rr = (i * TQ_ATTN + r0) % LANE
        e1, l1 = strip_softmax(slot, r0, qb, rr)
        e2, l2 = strip_softmax(slot, TQ_ATTN + r0, qb, rr)
        cc = lam * l1 * (1.0 / l2)
        r_ref[slot, r0:r0 + DA_STRIP, :] = jnp.broadcast_to(1.0 / l1, (DA_STRIP, LANE))
        return [x1 - cc * x2 for x1, x2 in zip(e1, e2)]

    def phase_b(i, slot):
        for r0 in range(0, TQ_ATTN, 2 * DA_STRIP):
            lo = strip_weights(i, slot, r0)
            hi = strip_weights(i, slot, r0 + DA_STRIP)
            for kb in range(n_kb):
                a_ref[slot, r0:r0 + 2 * DA_STRIP, kb * LANE:(kb + 1) * LANE] = (
                    jnp.concatenate([lo[kb], hi[kb]], axis=0).astype(BF16))

    def phase_c(i, slot):
        o = jnp.dot(a_ref[slot], v_ref[0], preferred_element_type=F32) * r_ref[slot]
        ms = jnp.mean(o * o, axis=-1, keepdims=True)
        o = o * lax.rsqrt(ms + EPS) * sg_ref[...] * (1.0 - lam_init)
        rows = slice(i * TQ_ATTN, (i + 1) * TQ_ATTN)
        o_ref[0, rows, :] = (o * _silu(ga_ref[0, rows, :])).astype(o_ref.dtype)

    phase_a(0, 0)
    for i in range(n_tiles):
        if i + 1 < n_tiles:
            phase_a(i + 1, (i + 1) % 2)
        phase_b(i, i % 2)
        phase_c(i, i % 2)


def _dattn(qkv, rest, tiles, tab_log2, lam_p, subln_g, lam_init):
    kern = functools.partial(_dattn_kernel, lam_init=lam_init)
    nh = DA_HEADS
    return pl.pallas_call(
        kern,
        grid=(nh, BATCH),
        in_specs=[
            pl.BlockSpec(memory_space=pltpu.SMEM),
            pl.BlockSpec((4, DA_HEAD_DIM), lambda h, b: (0, 0)),
            pl.BlockSpec((1, SEQ, LANE), lambda h, b: (b, 0, h)),
            pl.BlockSpec((1, SEQ, LANE), lambda h, b: (b, 0, nh + h)),
            pl.BlockSpec((1, SEQ, LANE), lambda h, b: (b, 0, 2 * nh + h)),
            pl.BlockSpec((1, 2 * T5_NEAR + 1, LANE, LANE), lambda h, b: (h, 0, 0, 0)),
            pl.BlockSpec((1, SEQ, LANE), lambda h, b: (b, 0, h)),
            pl.BlockSpec((1, LANE), lambda h, b: (0, 0)),
        ],
        out_specs=pl.BlockSpec((1, SEQ, LANE), lambda h, b: (b, 0, h)),
        out_shape=jax.ShapeDtypeStruct((BATCH, SEQ, DA_WIDTH), BF16),
        scratch_shapes=[
            pltpu.VMEM((2, 2 * TQ_ATTN, SEQ), F32),
            pltpu.VMEM((2, TQ_ATTN, SEQ), BF16),
            pltpu.VMEM((2, TQ_ATTN, LANE), F32),
        ],
        compiler_params=_cparams(),
        name="diff_attn",
    )(tab_log2, lam_p, qkv, qkv, qkv, tiles, rest, subln_g.reshape(1, LANE))


def _lru_kernel(x_ref, g_ref, cw_ref, cb_ref, w_ref, b_ref, lam_ref, o_ref,
                xc_ref, a3, u3, car_ref):
    S, C = SEQ, LRU_CH
    n_tiles = S // SUBLANE
    tiles_per_chunk = LRU_TC // SUBLANE

    x = x_ref[0]
    row = lax.broadcasted_iota(jnp.int32, (S, C), 0)
    xm2 = jnp.where(row >= 2, pltpu.roll(x, 2, 0), 0.0)
    xm1 = jnp.where(row >= 1, pltpu.roll(x, 1, 0), 0.0)
    xp1 = jnp.where(row < S - 1, pltpu.roll(x, S - 1, 0), 0.0)
    xc_ref[...] = (cw_ref[0, 0:1, :] * xm2 + cw_ref[0, 1:2, :] * xm1 + cw_ref[0, 2:3, :] * x
                   + cw_ref[0, 3:4, :] * xp1 + cb_ref[0])

    row8 = lax.broadcasted_iota(jnp.int32, (tiles_per_chunk, SUBLANE, C), 1)
    lam = lam_ref[0]
    half_c = (-0.5 * LRU_C) * (jnp.maximum(-lam, 0.0) + jnp.log1p(jnp.exp(-jnp.abs(lam))))

    def chunk_body(ci, carry):
        t0 = pl.multiple_of(ci * LRU_TC, LRU_TC)
        j0 = pl.multiple_of(ci * tiles_per_chunk, tiles_per_chunk)
        xc = xc_ref[pl.ds(t0, LRU_TC), :]
        xch = 0.5 * xc
        zh = jnp.dot(xc.astype(BF16), w_ref[0, 0], preferred_element_type=F32) + b_ref[0, 0]
        for direction in range(2):
            hc = half_c[direction:direction + 1, :]
            log_a = hc * jnp.tanh(zh[:, 2 * direction * C:(2 * direction + 1) * C]) + hc
            a = jnp.exp(log_a)
            m2 = -jnp.tanh(log_a) * (a * a + 1.0)
            mult = jnp.where(m2 > 0.0, m2 * lax.rsqrt(m2), 0.0)
            u = mult * (jnp.tanh(zh[:, (2 * direction + 1) * C:(2 * direction + 2) * C]) + 1.0) * xch
            a = a.reshape(tiles_per_chunk, SUBLANE, C)
            u = u.reshape(tiles_per_chunk, SUBLANE, C)
            for d in (1, 2, 4):
                if direction == 1:
                    ar = pltpu.roll(a, SUBLANE - d, 1)
                    ur = pltpu.roll(u, SUBLANE - d, 1)
                    m = row8 < SUBLANE - d
                else:
                    ar = pltpu.roll(a, d, 1)
                    ur = pltpu.roll(u, d, 1)
                    m = row8 >= d
                u = jnp.where(m, a * ur + u, u)
                a = jnp.where(m, a * ar, a)
            a3[direction, pl.ds(j0, tiles_per_chunk)] = a
            u3[direction, pl.ds(j0, tiles_per_chunk)] = u
        return carry

    lax.fori_loop(0, S // LRU_TC, chunk_body, 0)

    def carry_body(jf, c):
        cf, cb = c
        jb = n_tiles - 1 - jf
        car_ref[0, jf] = cf
        car_ref[1, jb] = cb
        last = SUBLANE - 1
        return (u3[0, jf, last:, :] + a3[0, jf, last:, :] * cf,
                u3[1, jb, 0:1, :] + a3[1, jb, 0:1, :] * cb)

    zero = jnp.zeros((1, C), F32)
    lax.fori_loop(0, n_tiles, carry_body, (zero, zero), unroll=8)

    hsum = (u3[0] + a3[0] * car_ref[0]) + (u3[1] + a3[1] * car_ref[1])
    o_ref[0] = (hsum.reshape(S, C) * _silu(g_ref[0])).astype(o_ref.dtype)


def _lru_weights(w_a, b_a, w_x, b_x):
    n_e = w_a.shape[0]
    n_grp = LRU_WIDTH // LRU_CH
    bpg = LRU_CH // LRU_BLOCK
    gates = jnp.stack([w_a[:, 0], w_x[:, 0], w_a[:, 1], w_x[:, 1]], axis=1)
    gates = gates.reshape(n_e, 4, n_grp, bpg, LRU_BLOCK, LRU_BLOCK)
    w = jnp.einsum('eqgncd,nm->egncqmd', gates, jnp.eye(bpg, dtype=gates.dtype))
    w = w.reshape(n_e, n_grp, LRU_CH, 4 * LRU_CH)
    b = jnp.stack([b_a[:, 0], b_x[:, 0], b_a[:, 1], b_x[:, 1]], axis=1)
    b = b.reshape(n_e, 4, n_grp, LRU_CH).transpose(0, 2, 1, 3).reshape(n_e, n_grp, 1, 4 * LRU_CH)
    return (0.5 * w).astype(BF16), 0.5 * b


def _lru(rest, e, conv_w, conv_b, w, b, lru_lambda):
    n_grp = LRU_WIDTH // LRU_CH
    xb0 = LRU_WIDTH // LRU_CH
    gb0 = 2 * LRU_WIDTH // LRU_CH
    return pl.pallas_call(
        _lru_kernel,
        grid=(BATCH, n_grp),
        in_specs=[
            pl.BlockSpec((1, SEQ, LRU_CH), lambda bb, c: (bb, 0, xb0 + c)),
            pl.BlockSpec((1, SEQ, LRU_CH), lambda bb, c: (bb, 0, gb0 + c)),
            pl.BlockSpec((1, 4, LRU_CH), lambda bb, c: (e, 0, c)),
            pl.BlockSpec((1, 1, LRU_CH), lambda bb, c: (e, 0, c)),
            pl.BlockSpec((1, 1, LRU_CH, 4 * LRU_CH), lambda bb, c: (e, c, 0, 0)),
            pl.BlockSpec((1, 1, 1, 4 * LRU_CH), lambda bb, c: (e, c, 0, 0)),
            pl.BlockSpec((1, 2, LRU_CH), lambda bb, c: (e, 0, c)),
        ],
        out_specs=pl.BlockSpec((1, SEQ, LRU_CH), lambda bb, c: (bb, 0, c)),
        out_shape=jax.ShapeDtypeStruct((BATCH, SEQ, LRU_WIDTH), BF16),
        scratch_shapes=[
            pltpu.VMEM((SEQ, LRU_CH), F32),
            pltpu.VMEM((2, SEQ // SUBLANE, SUBLANE, LRU_CH), F32),
            pltpu.VMEM((2, SEQ // SUBLANE, SUBLANE, LRU_CH), F32),
            pltpu.VMEM((2, SEQ // SUBLANE, 1, LRU_CH), F32),
        ],
        compiler_params=_cparams(),
        name="rglru",
    )(rest, rest, conv_w, conv_b.reshape(-1, 1, LRU_WIDTH), w, b, lru_lambda)


def _na_rpb_kernel(rpb_ref, dc_ref, o_ref):
    dc = dc_ref[...]
    n_dc = 2 * NA_WIN_C - 1
    val = jnp.broadcast_to(rpb_ref[:, 0:1], o_ref.shape)
    for j in range(1, n_dc):
        val = jnp.where(dc == j, rpb_ref[:, j:j + 1], val)
    o_ref[...] = val


def _na_bias(rpb):
    n_l = rpb.shape[0]
    n_dr = 2 * NA_WIN_R - 1
    n_dc = 2 * NA_WIN_C - 1
    cols = np.arange(GRID_W)
    dc_idx = (np.clip(cols[None, :] - cols[:, None], -(NA_WIN_C - 1), NA_WIN_C - 1)
              + NA_WIN_C - 1).astype(np.int32)
    toe = pl.pallas_call(
        _na_rpb_kernel,
        out_shape=jax.ShapeDtypeStruct((n_l * NA_HEADS * n_dr, GRID_W * GRID_W), F32),
        compiler_params=_cparams(),
        name="na_rpb",
    )(rpb.reshape(n_l * NA_HEADS * n_dr, n_dc), jnp.asarray(dc_idx.reshape(1, -1)))
    col_start = np.clip(cols - NA_WIN_C // 2, 0, GRID_W - NA_WIN_C)
    col_mask = (cols[None, :] >= col_start[:, None]) & (cols[None, :] < col_start[:, None] + NA_WIN_C)
    neg = jnp.asarray(np.where(col_mask, 0.0, -np.inf).astype(np.float32))
    toe = LOG2E * toe.reshape(n_l, NA_HEADS, n_dr, GRID_W, GRID_W) + neg
    tiles = jnp.concatenate([toe[:, :, :n_dr - 1], toe[:, :, 1:]], axis=-1)
    tiles = tiles.reshape(n_l, NA_HEADS // 2, 2, n_dr - 1, GRID_W, LANE)
    return tiles.transpose(0, 1, 3, 2, 4, 5).reshape(n_l, NA_HEADS // 2, n_dr - 1, 2 * GRID_W, LANE)


def _na_kernel(q_ref, k_ref, v_ref, g_ref, bias_ref, o_ref):
    win = NA_WIN_R * GRID_W
    lane = lax.broadcasted_iota(jnp.int32, (GRID_W, LANE), 1)
    lo = lane < NA_HEAD_DIM

    def group_body(gi, carry):
        q0s, vws, ss = [], [], []
        for i in range(NA_ROWS_PER_STEP):
            r = gi * NA_ROWS_PER_STEP + i
            rs = jnp.clip(r - NA_WIN_R // 2, 0, GRID_ROWS - NA_WIN_R)
            k0 = pl.multiple_of(rs * GRID_W, GRID_W)
            q0 = pl.multiple_of(r * GRID_W, GRID_W)
            q = q_ref[0, pl.ds(q0, GRID_W), :]
            zero = jnp.zeros_like(q)
            q2 = jnp.concatenate([jnp.where(lo, q, zero), jnp.where(lo, zero, q)], axis=0)
            kw = k_ref[0, pl.ds(k0, win), :]
            dr0 = NA_WIN_R - 1 - (r - rs)
            bias = jnp.concatenate([bias_ref[0, 0, dr0 + 2 * j] for j in range(NA_WIN_R // 2)], axis=1)
            ss.append(lax.dot_general(q2, kw, NT_DIMS, preferred_element_type=F32) + bias)
            vws.append(v_ref[0, pl.ds(k0, win), :])
            q0s.append(q0)
        es, rinv = [], []
        for s in ss:
            e = jnp.exp2(s - jnp.max(s, axis=-1, keepdims=True))
            rinv.append(1.0 / jnp.sum(e, axis=-1, keepdims=True))
            es.append(e.astype(BF16))
        pvs = [jnp.dot(e, vw, preferred_element_type=F32) for e, vw in zip(es, vws)]
        for pv, ri, q0 in zip(pvs, rinv, q0s):
            o2 = pv * ri
            o = jnp.where(lo, o2[:GRID_W], o2[GRID_W:])
            o_ref[0, pl.ds(q0, GRID_W), :] = (o * _silu(g_ref[0, pl.ds(q0, GRID_W), :])).astype(o_ref.dtype)
        return carry

    lax.fori_loop(0, GRID_ROWS // NA_ROWS_PER_STEP, group_body, 0)


def _na(qkv, gate, bias, o):
    n_pair = NA_HEADS // 2
    return pl.pallas_call(
        _na_kernel,
        grid=(BATCH, n_pair),
        in_specs=[
            pl.BlockSpec((1, SEQ, LANE), lambda b, p: (b, 0, p)),
            pl.BlockSpec((1, SEQ, LANE), lambda b, p: (b, 0, n_pair + p)),
            pl.BlockSpec((1, SEQ, LANE), lambda b, p: (b, 0, 2 * n_pair + p)),
            pl.BlockSpec((1, SEQ, LANE), lambda b, p: (b, 0, p)),
            pl.BlockSpec((1, 1, 2 * NA_WIN_R - 2, 2 * GRID_W, LANE), lambda b, p: (o, p, 0, 0, 0)),
        ],
        out_specs=pl.BlockSpec((1, SEQ, LANE), lambda b, p: (b, 0, p)),
        out_shape=jax.ShapeDtypeStruct((BATCH, SEQ, NA_WIDTH), BF16),
        compiler_params=_cparams(),
        name="nbr_attn",
    )(qkv, qkv, qkv, gate, bias)


def kernel(x, c, ada_w, ada_b, norm_g, final_g, t5_table, even_w_in, even_w_out, da_lam, da_subln_g,
           lru_conv_w, lru_conv_b, lru_w_a, lru_b_a, lru_w_x, lru_b_x, lru_lambda,
           odd_w_in, odd_w_out, na_rpb):
    assert x.shape == (BATCH, SEQ, D_MODEL) and x.dtype == F32 and c.shape == (BATCH, D_MODEL)
    assert even_w_in.shape == (DEPTH // 2, D_MODEL, 4 * DA_WIDTH + 2 * LRU_WIDTH)
    assert odd_w_in.shape == (DEPTH // 2, D_MODEL, 4 * NA_WIDTH)
    assert na_rpb.shape == (DEPTH // 2, NA_HEADS, 2 * NA_WIN_R - 1, 2 * NA_WIN_C - 1)
    mod = _ada_mod(c, ada_w, ada_b)
    tab_log2 = LOG2E * t5_table
    tiles = _t5_tiles(tab_log2)
    na_bias = _na_bias(na_rpb)
    lru_w, lru_b = _lru_weights(lru_w_a, lru_b_a, lru_w_x, lru_b_x)

    def w_in_bf16(w, n_q, head_dim):
        col_scale = np.ones((w.shape[-1],), np.float32)
        col_scale[:n_q] = head_dim ** -0.5 * LOG2E
        return (w * col_scale).astype(BF16)

    w_in = [w_in_bf16(even_w_in, DA_WIDTH, DA_HEAD_DIM), w_in_bf16(odd_w_in, NA_WIDTH, NA_HEAD_DIM)]
    w_out = [even_w_out.astype(BF16), odd_w_out.astype(BF16)]
    widths_in = [(3 * DA_WIDTH, DA_WIDTH + 2 * LRU_WIDTH), (3 * NA_WIDTH, NA_WIDTH)]

    def in_args(l):
        n1, n2 = widths_in[l % 2]
        return dict(w_in=w_in[l % 2], norm_g=norm_g, layer_in=l, idx_in=l // 2, n1=n1, n2=n2)

    act, rest = _proj(x, mod, **in_args(0))
    for l in range(DEPTH):
        i = l // 2
        if l % 2 == 0:
            lam_init = 0.8 - 0.6 * math.exp(-0.3 * l)
            mixed = [_dattn(act, rest, tiles, tab_log2, da_lam[i], da_subln_g[i], lam_init),
                     _lru(rest, i, lru_conv_w, lru_conv_b, lru_w, lru_b, lru_lambda)]
        else:
            mixed = [_na(act, rest, na_bias, i)]
        out_args = dict(mixed=mixed, w_out=w_out[l % 2], layer_out=l, idx_out=i)
        if l + 1 < DEPTH:
            x, act, rest = _proj(x, mod, **out_args, **in_args(l + 1))
        else:
            (x,) = _proj(x, mod, **out_args, final_g=final_g)
    return x
```

```python
import functools
import math

import numpy as np
import jax
import jax.numpy as jnp
from jax import lax
from jax.experimental import pallas as pl
from jax.experimental.pallas import tpu as pltpu

F32 = jnp.float32
BF16 = jnp.bfloat16

D_MODEL = 1024
BATCH = 16
SEQ = 2048
DEPTH = 4
GRID_W = 64
EPS = 1e-6

DA_HEADS = 4
DA_HEAD_DIM = 64
DA_WIDTH = DA_HEADS * 2 * DA_HEAD_DIM
T5_BUCKETS = 32
T5_MAX_DIST = 128
LRU_WIDTH = D_MODEL // 2
LRU_BLOCKS = 8
LRU_BLOCK = LRU_WIDTH // LRU_BLOCKS
LRU_C = 8.0
NA_HEADS = 16
NA_HEAD_DIM = 64
NA_WIDTH = NA_HEADS * NA_HEAD_DIM
NA_WIN_R = 8
NA_WIN_C = 16
GRID_ROWS = SEQ // GRID_W

LANE = 128
SUBLANE = 8
VMEM_LIMIT = 48 * 1024 * 1024

TM_PROJ = 512
TN_PROJ = 512
TQ_ATTN = 512
LRU_CH = 256
LRU_TC = 1024
N_TILE_DIAG = 2 * (SEQ // LANE) - 1
T5_NEAR = 1
DA_STRIP = SUBLANE
LOG2E = math.log2(math.e)
NA_ROWS_PER_STEP = 16

NT_DIMS = (((1,), (1,)), ((), ()))


def _cparams():
    return pltpu.CompilerParams(vmem_limit_bytes=VMEM_LIMIT)


def _silu(x):
    return x * jax.nn.sigmoid(x)


def _ada_kernel(c_ref, w_ref, b_ref, o_ref):
    ca = _silu(c_ref[...])
    o_ref[0, 0] = jnp.dot(ca, w_ref[0], preferred_element_type=F32) + b_ref[0, 0]


def _ada_mod(c, ada_w, ada_b):
    return pl.pallas_call(
        _ada_kernel,
        grid=(DEPTH, 3),
        in_specs=[
            pl.BlockSpec((BATCH, D_MODEL), lambda l, j: (0, 0)),
            pl.BlockSpec((1, D_MODEL, D_MODEL), lambda l, j: (l, 0, j)),
            pl.BlockSpec((1, 1, 1, D_MODEL), lambda l, j: (l, j, 0, 0)),
        ],
        out_specs=pl.BlockSpec((1, 1, BATCH, D_MODEL), lambda l, j: (l, j, 0, 0)),
        out_shape=jax.ShapeDtypeStruct((DEPTH, 3, BATCH, D_MODEL), F32),
        compiler_params=_cparams(),
        name="ada_mod",
    )(c, ada_w, ada_b.reshape(DEPTH, 3, 1, D_MODEL))


def _rms(x, g):
    return x * lax.rsqrt(jnp.mean(x * x, axis=-1, keepdims=True) + EPS) * g


def _proj_kernel(*refs, widths, has_in, final, n1, n2, chunk):
    it = iter(refs)
    b = pl.program_id(0)
    m_refs = [next(it) for _ in widths]
    if widths:
        wo_ref, mod_ref = next(it), next(it)
    x_ref = next(it)
    if has_in:
        modn_ref, g_ref, wi_ref = next(it), next(it), next(it)
    if final:
        fg_ref = next(it)
    if widths:
        xo_ref = next(it)
    if has_in:
        o1_ref, o2_ref = next(it), next(it)

    x = x_ref[0]
    if widths:
        y, off = None, 0
        for m_ref, wd in zip(m_refs, widths):
            part = jnp.dot(m_ref[0], wo_ref[0, off:off + wd, :], preferred_element_type=F32)
            y = part if y is None else y + part
            off += wd
        x = x + mod_ref[0, 2, pl.ds(b, 1), :] * y
        xo_ref[0] = _rms(x, fg_ref[...]) if final else x
    if has_in:
        y = _rms(x, g_ref[0])
        h = (y * (1.0 + modn_ref[0, 1, pl.ds(b, 1), :]) + modn_ref[0, 0, pl.ds(b, 1), :]).astype(BF16)
        for j in range(0, n1, chunk):
            o1_ref[0, :, j:j + chunk] = jnp.dot(
                h, wi_ref[0, :, j:j + chunk], preferred_element_type=F32).astype(o1_ref.dtype)
        for j in range(0, n2, chunk):
            o2_ref[0, :, j:j + chunk] = jnp.dot(
                h, wi_ref[0, :, n1 + j:n1 + j + chunk], preferred_element_type=F32).astype(o2_ref.dtype)


def _proj(x, mod, *, mixed=(), w_out=None, layer_out=None, w_in=None, norm_g=None, layer_in=None,
          idx_out=0, idx_in=0, n1=0, n2=0, final_g=None):
    widths = tuple(m.shape[-1] for m in mixed)
    has_in = w_in is not None
    final = final_g is not None
    kern = functools.partial(_proj_kernel, widths=widths, has_in=has_in, final=final, n1=n1, n2=n2,
                             chunk=TN_PROJ)
    tile = lambda wd: pl.BlockSpec((1, TM_PROJ, wd), lambda b, i: (b, i, 0))
    const = lambda shape, first: pl.BlockSpec(shape, lambda b, i: (first,) + (0,) * (len(shape) - 1),
                                              pipeline_mode=pl.Buffered(1))
    mod_spec = lambda layer: const((1, 3, BATCH, D_MODEL), layer)
    in_specs, args = [tile(wd) for wd in widths], list(mixed)
    if widths:
        in_specs += [const((1, sum(widths), D_MODEL), idx_out), mod_spec(layer_out)]
        args += [w_out, mod]
    in_specs.append(tile(D_MODEL))
    args.append(x)
    if has_in:
        in_specs += [mod_spec(layer_in), const((1, 1, D_MODEL), layer_in), const((1, D_MODEL, n1 + n2), idx_in)]
        args += [mod, norm_g.reshape(DEPTH, 1, D_MODEL), w_in]
    if final:
        in_specs.append(const((1, D_MODEL), 0))
        args.append(final_g.reshape(1, D_MODEL))
    out_specs, out_shape = [], []
    if widths:
        out_specs.append(tile(D_MODEL))
        out_shape.append(jax.ShapeDtypeStruct((BATCH, SEQ, D_MODEL), F32))
    if has_in:
        out_specs += [tile(n1), tile(n2)]
        out_shape += [jax.ShapeDtypeStruct((BATCH, SEQ, n1), BF16), jax.ShapeDtypeStruct((BATCH, SEQ, n2), F32)]
    return pl.pallas_call(
        kern,
        grid=(BATCH, SEQ // TM_PROJ),
        in_specs=in_specs,
        out_specs=out_specs,
        out_shape=out_shape,
        compiler_params=_cparams(),
        name="proj_" + ("o" if widths else "") + ("i" if has_in else "") + ("f" if final else ""),
    )(*args)


def _t5_bucket_np(rel):
    nb = T5_BUCKETS // 2
    max_exact = nb // 2
    ret = np.where(rel > 0, nb, 0)
    n = np.abs(rel)
    nf = np.maximum(n, 1).astype(np.float32)
    large = max_exact + (np.log(nf / np.float32(max_exact)) / np.float32(math.log(T5_MAX_DIST / max_exact))
                         * np.float32(nb - max_exact)).astype(np.int32)
    large = np.minimum(large, nb - 1)
    return (ret + np.where(n < max_exact, n, large)).astype(np.int32)


def _t5_tile_buckets():
    d = np.arange(N_TILE_DIAG)[:, None, None] - (SEQ // LANE - 1)
    i = np.arange(LANE)[None, :, None]
    j = np.arange(LANE)[None, None, :]
    idx = _t5_bucket_np(d * LANE + j - i)
    mid = SEQ // LANE - 1
    assert (idx[:mid - T5_NEAR] == T5_BUCKETS // 2 - 1).all() and (idx[mid + T5_NEAR + 1:] == T5_BUCKETS - 1).all()
    return idx[mid - T5_NEAR:mid + T5_NEAR + 1]


def _t5_tiles_kernel(tab_ref, idx_ref, o_ref):
    h = pl.program_id(0)
    for d in range(2 * T5_NEAR + 1):
        idx = idx_ref[d]
        val = jnp.full(idx.shape, tab_ref[h, 0], F32)
        for j in range(1, T5_BUCKETS):
            val = jnp.where(idx == j, tab_ref[h, j], val)
        o_ref[0, d] = val


def _t5_tiles(t5_table):
    idx = jnp.asarray(_t5_tile_buckets())
    n_near = 2 * T5_NEAR + 1
    return pl.pallas_call(
        _t5_tiles_kernel,
        grid=(DA_HEADS,),
        in_specs=[
            pl.BlockSpec(memory_space=pltpu.SMEM),
            pl.BlockSpec((n_near, LANE, LANE), lambda h: (0, 0, 0)),
        ],
        out_specs=pl.BlockSpec((1, n_near, LANE, LANE), lambda h: (h, 0, 0, 0)),
        out_shape=jax.ShapeDtypeStruct((DA_HEADS, n_near, LANE, LANE), F32),
        compiler_params=_cparams(),
        name="t5_tiles",
    )(t5_table, idx)


def _dattn_kernel(tab_ref, lam_ref, q_ref, k_ref, v_ref, t_ref, ga_ref, sg_ref, o_ref,
                  s_ref, a_ref, r_ref, *, lam_init):
    h = pl.program_id(0)
    n_kb = SEQ // LANE
    n_tiles = SEQ // TQ_ATTN
    c_left = tab_ref[h, T5_BUCKETS // 2 - 1]
    c_right = tab_ref[h, T5_BUCKETS - 1]

    lp = lam_ref[...]
    lam = (jnp.exp(jnp.sum(lp[0:1] * lp[1:2], axis=-1, keepdims=True))
           - jnp.exp(jnp.sum(lp[2:3] * lp[3:4], axis=-1, keepdims=True)) + lam_init)
    lane = lax.broadcasted_iota(jnp.int32, (TQ_ATTN, LANE), 1)

    def phase_a(i, slot):
        q = q_ref[0, i * TQ_ATTN:(i + 1) * TQ_ATTN, :]
        zero = jnp.zeros_like(q)
        qst = jnp.concatenate([jnp.where(lane < DA_HEAD_DIM, q, zero),
                               jnp.where(lane >= DA_HEAD_DIM, q, zero)], axis=0)
        s_ref[slot] = lax.dot_general(qst, k_ref[0], NT_DIMS, preferred_element_type=F32)

    def strip_softmax(slot, row0, qb, rr):
        cols, m_left, m_right, m_near = [], None, None, None
        for kb in range(n_kb):
            x = s_ref[slot, row0:row0 + DA_STRIP, kb * LANE:(kb + 1) * LANE]
            d = kb - qb
            if abs(d) <= T5_NEAR:
                x = x + t_ref[0, d + T5_NEAR, rr:rr + DA_STRIP, :]
                m_near = x if m_near is None else jnp.maximum(m_near, x)
            elif d < 0:
                m_left = x if m_left is None else jnp.maximum(m_left, x)
            else:
                m_right = x if m_right is None else jnp.maximum(m_right, x)
            cols.append(x)
        m = m_near
        if m_left is not None:
            m = jnp.maximum(m, m_left + c_left)
        if m_right is not None:
            m = jnp.maximum(m, m_right + c_right)
        m = jnp.max(m, axis=-1, keepdims=True)
        off_left, off_right = m - c_left, m - c_right
        es, tot = [], None
        for kb in range(n_kb):
            d = kb - qb
            off = m if abs(d) <= T5_NEAR else (off_left if d < 0 else off_right)
            e = jnp.exp2(cols[kb] - off)
            tot = e if tot is None else tot + e
            es.append(e)
        return es, jnp.sum(tot, axis=-1, keepdims=True)

    def strip_weights(i, slot, r0):
        qb = (i * TQ_ATTN + r0) // LANE
        rr = (i * TQ_ATTN + r0) % LANE
        e1, l1 = strip_softmax(slot, r0, qb, rr)
        e2, l2 = strip_softmax(slot, TQ_ATTN + r0, qb, rr)
        cc = lam * l1 * (1.0 / l2)
        r_ref[slot, r0:r0 + DA_STRIP, :] = jnp.broadcast_to(1.0 / l1, (DA_STRIP, LANE))
        return [x1 - cc * x2 for x1, x2 in zip(e1, e2)]

    def phase_b(i, slot):
        for r0 in range(0, TQ_ATTN, 2 * DA_STRIP):
            lo = strip_weights(i, slot, r0)
            hi = strip_weights(i, slot, r0 + DA_STRIP)
            for kb in range(n_kb):
                a_ref[slot, r0:r0 + 2 * DA_STRIP, kb * LANE:(kb + 1) * LANE] = (
                    jnp.concatenate([lo[kb], hi[kb]], axis=0).astype(BF16))

    def phase_c(i, slot):
        o = jnp.dot(a_ref[slot], v_ref[0], preferred_element_type=F32) * r_ref[slot]
        ms = jnp.mean(o * o, axis=-1, keepdims=True)
        o = o * lax.rsqrt(ms + EPS) * sg_ref[...] * (1.0 - lam_init)
        rows = slice(i * TQ_ATTN, (i + 1) * TQ_ATTN)
        o_ref[0, rows, :] = (o * _silu(ga_ref[0, rows, :])).astype(o_ref.dtype)

    phase_a(0, 0)
    for i in range(n_tiles):
        if i + 1 < n_tiles:
            phase_a(i + 1, (i + 1) % 2)
        phase_b(i, i % 2)
        phase_c(i, i % 2)


def _dattn(qkv, rest, tiles, tab_log2, lam_p, subln_g, lam_init):
    kern = functools.partial(_dattn_kernel, lam_init=lam_init)
    nh = DA_HEADS
    return pl.pallas_call(
        kern,
        grid=(nh, BATCH),
        in_specs=[
            pl.BlockSpec(memory_space=pltpu.SMEM),
            pl.BlockSpec((4, DA_HEAD_DIM), lambda h, b: (0, 0)),
            pl.BlockSpec((1, SEQ, LANE), lambda h, b: (b, 0, h)),
            pl.BlockSpec((1, SEQ, LANE), lambda h, b: (b, 0, nh + h)),
            pl.BlockSpec((1, SEQ, LANE), lambda h, b: (b, 0, 2 * nh + h)),
            pl.BlockSpec((1, 2 * T5_NEAR + 1, LANE, LANE), lambda h, b: (h, 0, 0, 0)),
            pl.BlockSpec((1, SEQ, LANE), lambda h, b: (b, 0, h)),
            pl.BlockSpec((1, LANE), lambda h, b: (0, 0)),
        ],
        out_specs=pl.BlockSpec((1, SEQ, LANE), lambda h, b: (b, 0, h)),
        out_shape=jax.ShapeDtypeStruct((BATCH, SEQ, DA_WIDTH), BF16),
        scratch_shapes=[
            pltpu.VMEM((2, 2 * TQ_ATTN, SEQ), F32),
            pltpu.VMEM((2, TQ_ATTN, SEQ), BF16),
            pltpu.VMEM((2, TQ_ATTN, LANE), F32),
        ],
        compiler_params=_cparams(),
        name="diff_attn",
    )(tab_log2, lam_p, qkv, qkv, qkv, tiles, rest, subln_g.reshape(1, LANE))


def _lru_kernel(x_ref, g_ref, cw_ref, cb_ref, w_ref, b_ref, lam_ref, o_ref,
                xc_ref, a3, u3, car_ref):
    S, C = SEQ, LRU_CH
    n_tiles = S // SUBLANE
    tiles_per_chunk = LRU_TC // SUBLANE

    x = x_ref[0]
    row = lax.broadcasted_iota(jnp.int32, (S, C), 0)
    xm2 = jnp.where(row >= 2, pltpu.roll(x, 2, 0), 0.0)
    xm1 = jnp.where(row >= 1, pltpu.roll(x, 1, 0), 0.0)
    xp1 = jnp.where(row < S - 1, pltpu.roll(x, S - 1, 0), 0.0)
    xc_ref[...] = (cw_ref[0, 0:1, :] * xm2 + cw_ref[0, 1:2, :] * xm1 + cw_ref[0, 2:3, :] * x
                   + cw_ref[0, 3:4, :] * xp1 + cb_ref[0])

    row8 = lax.broadcasted_iota(jnp.int32, (tiles_per_chunk, SUBLANE, C), 1)
    lam = lam_ref[0]
    half_c = (-0.5 * LRU_C) * (jnp.maximum(-lam, 0.0) + jnp.log1p(jnp.exp(-jnp.abs(lam))))

    def chunk_body(ci, carry):
        t0 = pl.multiple_of(ci * LRU_TC, LRU_TC)
        j0 = pl.multiple_of(ci * tiles_per_chunk, tiles_per_chunk)
        xc = xc_ref[pl.ds(t0, LRU_TC), :]
        xch = 0.5 * xc
        zh = jnp.dot(xc.astype(BF16), w_ref[0, 0], preferred_element_type=F32) + b_ref[0, 0]
        for direction in range(2):
            hc = half_c[direction:direction + 1, :]
            log_a = hc * jnp.tanh(zh[:, 2 * direction * C:(2 * direction + 1) * C]) + hc
            a = jnp.exp(log_a)
            m2 = -jnp.tanh(log_a) * (a * a + 1.0)
            mult = jnp.where(m2 > 0.0, m2 * lax.rsqrt(m2), 0.0)
            u = mult * (jnp.tanh(zh[:, (2 * direction + 1) * C:(2 * direction + 2) * C]) + 1.0) * xch
            a = a.reshape(tiles_per_chunk, SUBLANE, C)
            u = u.reshape(tiles_per_chunk, SUBLANE, C)
            for d in (1, 2, 4):
                if direction == 1:
                    ar = pltpu.roll(a, SUBLANE - d, 1)
                    ur = pltpu.roll(u, SUBLANE - d, 1)
                    m = row8 < SUBLANE - d
                else:
                    ar = pltpu.roll(a, d, 1)
                    ur = pltpu.roll(u, d, 1)
                    m = row8 >= d
                u = jnp.where(m, a * ur + u, u)
                a = jnp.where(m, a * ar, a)
            a3[direction, pl.ds(j0, tiles_per_chunk)] = a
            u3[direction, pl.ds(j0, tiles_per_chunk)] = u
        return carry

    lax.fori_loop(0, S // LRU_TC, chunk_body, 0)

    def carry_body(jf, c):
        cf, cb = c
        jb = n_tiles - 1 - jf
        car_ref[0, jf] = cf
        car_ref[1, jb] = cb
        last = SUBLANE - 1
        return (u3[0, jf, last:, :] + a3[0, jf, last:, :] * cf,
                u3[1, jb, 0:1, :] + a3[1, jb, 0:1, :] * cb)

    zero = jnp.zeros((1, C), F32)
    lax.fori_loop(0, n_tiles, carry_body, (zero, zero), unroll=8)

    hsum = (u3[0] + a3[0] * car_ref[0]) + (u3[1] + a3[1] * car_ref[1])
    o_ref[0] = (hsum.reshape(S, C) * _silu(g_ref[0])).astype(o_ref.dtype)


def _lru_weights(w_a, b_a, w_x, b_x):
    n_e = w_a.shape[0]
    n_grp = LRU_WIDTH // LRU_CH
    bpg = LRU_CH // LRU_BLOCK
    gates = jnp.stack([w_a[:, 0], w_x[:, 0], w_a[:, 1], w_x[:, 1]], axis=1)
    gates = gates.reshape(n_e, 4, n_grp, bpg, LRU_BLOCK, LRU_BLOCK)
    w = jnp.einsum('eqgncd,nm->egncqmd', gates, jnp.eye(bpg, dtype=gates.dtype))
    w = w.reshape(n_e, n_grp, LRU_CH, 4 * LRU_CH)
    b = jnp.stack([b_a[:, 0], b_x[:, 0], b_a[:, 1], b_x[:, 1]], axis=1)
    b = b.reshape(n_e, 4, n_grp, LRU_CH).transpose(0, 2, 1, 3).reshape(n_e, n_grp, 1, 4 * LRU_CH)
    return (0.5 * w).astype(BF16), 0.5 * b


def _lru(rest, e, conv_w, conv_b, w, b, lru_lambda):
    n_grp = LRU_WIDTH // LRU_CH
    xb0 = LRU_WIDTH // LRU_CH
    gb0 = 2 * LRU_WIDTH // LRU_CH
    return pl.pallas_call(
        _lru_kernel,
        grid=(BATCH, n_grp),
        in_specs=[
            pl.BlockSpec((1, SEQ, LRU_CH), lambda bb, c: (bb, 0, xb0 + c)),
            pl.BlockSpec((1, SEQ, LRU_CH), lambda bb, c: (bb, 0, gb0 + c)),
            pl.BlockSpec((1, 4, LRU_CH), lambda bb, c: (e, 0, c)),
            pl.BlockSpec((1, 1, LRU_CH), lambda bb, c: (e, 0, c)),
            pl.BlockSpec((1, 1, LRU_CH, 4 * LRU_CH), lambda bb, c: (e, c, 0, 0)),
            pl.BlockSpec((1, 1, 1, 4 * LRU_CH), lambda bb, c: (e, c, 0, 0)),
            pl.BlockSpec((1, 2, LRU_CH), lambda bb, c: (e, 0, c)),
        ],
        out_specs=pl.BlockSpec((1, SEQ, LRU_CH), lambda bb, c: (bb, 0, c)),
        out_shape=jax.ShapeDtypeStruct((BATCH, SEQ, LRU_WIDTH), BF16),
        scratch_shapes=[
            pltpu.VMEM((SEQ, LRU_CH), F32),
            pltpu.VMEM((2, SEQ // SUBLANE, SUBLANE, LRU_CH), F32),
            pltpu.VMEM((2, SEQ // SUBLANE, SUBLANE, LRU_CH), F32),
            pltpu.VMEM((2, SEQ // SUBLANE, 1, LRU_CH), F32),
        ],
        compiler_params=_cparams(),
        name="rglru",
    )(rest, rest, conv_w, conv_b.reshape(-1, 1, LRU_WIDTH), w, b, lru_lambda)


def _na_rpb_kernel(rpb_ref, dc_ref, o_ref):
    dc = dc_ref[...]
    n_dc = 2 * NA_WIN_C - 1
    val = jnp.broadcast_to(rpb_ref[:, 0:1], o_ref.shape)
    for j in range(1, n_dc):
        val = jnp.where(dc == j, rpb_ref[:, j:j + 1], val)
    o_ref[...] = val


def _na_bias(rpb):
    n_l = rpb.shape[0]
    n_dr = 2 * NA_WIN_R - 1
    n_dc = 2 * NA_WIN_C - 1
    cols = np.arange(GRID_W)
    dc_idx = (np.clip(cols[None, :] - cols[:, None], -(NA_WIN_C - 1), NA_WIN_C - 1)
              + NA_WIN_C - 1).astype(np.int32)
    toe = pl.pallas_call(
        _na_rpb_kernel,
        out_shape=jax.ShapeDtypeStruct((n_l * NA_HEADS * n_dr, GRID_W * GRID_W), F32),
        compiler_params=_cparams(),
        name="na_rpb",
    )(rpb.reshape(n_l * NA_HEADS * n_dr, n_dc), jnp.asarray(dc_idx.reshape(1, -1)))
    col_start = np.clip(cols - NA_WIN_C // 2, 0, GRID_W - NA_WIN_C)
    col_mask = (cols[None, :] >= col_start[:, None]) & (cols[None, :] < col_start[:, None] + NA_WIN_C)
    neg = jnp.asarray(np.where(col_mask, 0.0, -np.inf).astype(np.float32))
    toe = LOG2E * toe.reshape(n_l, NA_HEADS, n_dr, GRID_W, GRID_W) + neg
    tiles = jnp.concatenate([toe[:, :, :n_dr - 1], toe[:, :, 1:]], axis=-1)
    tiles = tiles.reshape(n_l, NA_HEADS // 2, 2, n_dr - 1, GRID_W, LANE)
    return tiles.transpose(0, 1, 3, 2, 4, 5).reshape(n_l, NA_HEADS // 2, n_dr - 1, 2 * GRID_W, LANE)


def _na_kernel(q_ref, k_ref, v_ref, g_ref, bias_ref, o_ref):
    win = NA_WIN_R * GRID_W
    lane = lax.broadcasted_iota(jnp.int32, (GRID_W, LANE), 1)
    lo = lane < NA_HEAD_DIM

    def group_body(gi, carry):
        q0s, vws, ss = [], [], []
        for i in range(NA_ROWS_PER_STEP):
            r = gi * NA_ROWS_PER_STEP + i
            rs = jnp.clip(r - NA_WIN_R // 2, 0, GRID_ROWS - NA_WIN_R)
            k0 = pl.multiple_of(rs * GRID_W, GRID_W)
            q0 = pl.multiple_of(r * GRID_W, GRID_W)
            q = q_ref[0, pl.ds(q0, GRID_W), :]
            zero = jnp.zeros_like(q)
            q2 = jnp.concatenate([jnp.where(lo, q, zero), jnp.where(lo, zero, q)], axis=0)
            kw = k_ref[0, pl.ds(k0, win), :]
            dr0 = NA_WIN_R - 1 - (r - rs)
            bias = jnp.concatenate([bias_ref[0, 0, dr0 + 2 * j] for j in range(NA_WIN_R // 2)], axis=1)
            ss.append(lax.dot_general(q2, kw, NT_DIMS, preferred_element_type=F32) + bias)
            vws.append(v_ref[0, pl.ds(k0, win), :])
            q0s.append(q0)
        es, rinv = [], []
        for s in ss:
            e = jnp.exp2(s - jnp.max(s, axis=-1, keepdims=True))
            rinv.append(1.0 / jnp.sum(e, axis=-1, keepdims=True))
            es.append(e.astype(BF16))
        pvs = [jnp.dot(e, vw, preferred_element_type=F32) for e, vw in zip(es, vws)]
        for pv, ri, q0 in zip(pvs, rinv, q0s):
            o2 = pv * ri
            o = jnp.where(lo, o2[:GRID_W], o2[GRID_W:])
            o_ref[0, pl.ds(q0, GRID_W), :] = (o * _silu(g_ref[0, pl.ds(q0, GRID_W), :])).astype(o_ref.dtype)
        return carry

    lax.fori_loop(0, GRID_ROWS // NA_ROWS_PER_STEP, group_body, 0)


def _na(qkv, gate, bias, o):
    n_pair = NA_HEADS // 2
    return pl.pallas_call(
        _na_kernel,
        grid=(BATCH, n_pair),
        in_specs=[
            pl.BlockSpec((1, SEQ, LANE), lambda b, p: (b, 0, p)),
            pl.BlockSpec((1, SEQ, LANE), lambda b, p: (b, 0, n_pair + p)),
            pl.BlockSpec((1, SEQ, LANE), lambda b, p: (b, 0, 2 * n_pair + p)),
            pl.BlockSpec((1, SEQ, LANE), lambda b, p: (b, 0, p)),
            pl.BlockSpec((1, 1, 2 * NA_WIN_R - 2, 2 * GRID_W, LANE), lambda b, p: (o, p, 0, 0, 0)),
        ],
        out_specs=pl.BlockSpec((1, SEQ, LANE), lambda b, p: (b, 0, p)),
        out_shape=jax.ShapeDtypeStruct((BATCH, SEQ, NA_WIDTH), BF16),
        compiler_params=_cparams(),
        name="nbr_attn",
    )(qkv, qkv, qkv, gate, bias)


def kernel(x, c, ada_w, ada_b, norm_g, final_g, t5_table, even_w_in, even_w_out, da_lam, da_subln_g,
           lru_conv_w, lru_conv_b, lru_w_a, lru_b_a, lru_w_x, lru_b_x, lru_lambda,
           odd_w_in, odd_w_out, na_rpb):
    assert x.shape == (BATCH, SEQ, D_MODEL) and x.dtype == F32 and c.shape == (BATCH, D_MODEL)
    assert even_w_in.shape == (DEPTH // 2, D_MODEL, 4 * DA_WIDTH + 2 * LRU_WIDTH)
    assert odd_w_in.shape == (DEPTH // 2, D_MODEL, 4 * NA_WIDTH)
    assert na_rpb.shape == (DEPTH // 2, NA_HEADS, 2 * NA_WIN_R - 1, 2 * NA_WIN_C - 1)
    mod = _ada_mod(c, ada_w, ada_b)
    tab_log2 = LOG2E * t5_table
    tiles = _t5_tiles(tab_log2)
    na_bias = _na_bias(na_rpb)
    lru_w, lru_b = _lru_weights(lru_w_a, lru_b_a, lru_w_x, lru_b_x)

    def w_in_bf16(w, n_q, head_dim):
        col_scale = np.ones((w.shape[-1],), np.float32)
        col_scale[:n_q] = head_dim ** -0.5 * LOG2E
        return (w * col_scale).astype(BF16)

    w_in = [w_in_bf16(even_w_in, DA_WIDTH, DA_HEAD_DIM), w_in_bf16(odd_w_in, NA_WIDTH, NA_HEAD_DIM)]
    w_out = [even_w_out.astype(BF16), odd_w_out.astype(BF16)]
    widths_in = [(3 * DA_WIDTH, DA_WIDTH + 2 * LRU_WIDTH), (3 * NA_WIDTH, NA_WIDTH)]

    def in_args(l):
        n1, n2 = widths_in[l % 2]
        return dict(w_in=w_in[l % 2], norm_g=norm_g, layer_in=l, idx_in=l // 2, n1=n1, n2=n2)

    act, rest = _proj(x, mod, **in_args(0))
    for l in range(DEPTH):
        i = l // 2
        if l % 2 == 0:
            lam_init = 0.8 - 0.6 * math.exp(-0.3 * l)
            mixed = [_dattn(act, rest, tiles, tab_log2, da_lam[i], da_subln_g[i], lam_init),
                     _lru(rest, i, lru_conv_w, lru_conv_b, lru_w, lru_b, lru_lambda)]
        else:
            mixed = [_na(act, rest, na_bias, i)]
        out_args = dict(mixed=mixed, w_out=w_out[l % 2], layer_out=l, idx_out=i)
        if l + 1 < DEPTH:
            x, act, rest = _proj(x, mod, **out_args, **in_args(l + 1))
        else:
            (x,) = _proj(x, mod, **out_args, final_g=final_g)
    return x
```
